```python
import math
import jax, jax.numpy as jnp
from jax import lax
import numpy as np

D_MODEL = 2048
BATCH = 4
SEQ = 2048
DEPTH = 4
DEC_BATCH = 8
DEC_SEQ = 4
PAST_LEN = 16384
PAGE_SIZE = 128

DA_WIDTH = D_MODEL // 2
RW_WIDTH = D_MODEL - DA_WIDTH
DA_V_DIM = 128
N_DA_HEADS = DA_WIDTH // DA_V_DIM
DA_QK_DIM = DA_V_DIM // 2
N_DA_SUB = 2 * N_DA_HEADS
RW_HEAD = 64
N_RW_HEADS = RW_WIDTH // RW_HEAD
W_LORA = 64
A_LORA = 64
G_LORA = 160
N_RW_COLS = 3 * RW_WIDTH + W_LORA + A_LORA + G_LORA
N_IN = 3 * DA_WIDTH + N_RW_COLS
D_FF = ((8 * D_MODEL + 3 * 256 - 1) // (3 * 256)) * 256
ROPE_THETA = 10000.0
NORM_EPS = 1e-6
LNX_EPS = 64e-5
Q_BLOCK = 128
MASK_VALUE = -1e30
DA_SCALE = DA_QK_DIM ** -0.5

kernel_name = 'hybrid_diffattn_rwkv7_decoder_step'


def _rmsnorm(x, g):
    xf = x.astype(jnp.float32)
    y = xf * lax.rsqrt(jnp.mean(xf * xf, axis=-1, keepdims=True) + NORM_EPS)
    return (y * g.astype(jnp.float32)).astype(x.dtype)


def _rope(x, pos):
    half = DA_QK_DIM // 2
    inv_freq = jnp.power(ROPE_THETA, -jnp.arange(half, dtype=jnp.float32) / half)
    ang = pos.astype(jnp.float32)[:, None] * inv_freq[None, :]
    cos = jnp.cos(ang)[None, :, None, :]
    sin = jnp.sin(ang)[None, :, None, :]
    xf = x.astype(jnp.float32)
    x1, x2 = xf[..., :half], xf[..., half:]
    return jnp.concatenate([x1 * cos - x2 * sin, x2 * cos + x1 * sin], axis=-1).astype(x.dtype)


def _da_combine(s, lam):
    p = jax.nn.softmax(s, axis=-1)
    B, _, Tq, Tk = p.shape
    p = p.reshape(B, N_DA_HEADS, 2, Tq, Tk)
    return p[:, :, 0] - lam * p[:, :, 1]


def _da_prompt(q, k, v, lam):
    B, S = q.shape[:2]
    nblk = S // Q_BLOCK
    qb = jnp.moveaxis(q.reshape(B, nblk, Q_BLOCK, N_DA_SUB, DA_QK_DIM), 1, 0)
    kpos = jnp.arange(S, dtype=jnp.int32)
    qpos = kpos.reshape(nblk, Q_BLOCK)
    kf = k.astype(jnp.float32)
    vf = v.astype(jnp.float32)

    def block(args):
        qi, pi = args
        s = jnp.einsum('bqhd,bkhd->bhqk', qi.astype(jnp.float32), kf) * DA_SCALE
        s = jnp.where(kpos[None, :] <= pi[:, None], s, MASK_VALUE)
        a = _da_combine(s, lam)
        return jnp.einsum('bhqk,bkhd->bqhd', a, vf)

    o = lax.map(block, (qb, qpos))
    return jnp.moveaxis(o, 0, 1).reshape(B, S, N_DA_HEADS, DA_V_DIM)


def _da_sample(q, k_new, v_new, k_past, v_past, lam):
    T = q.shape[1]
    P = k_past.shape[1]
    qf = q.astype(jnp.float32)
    s_past = jnp.einsum('bqhd,bkhd->bhqk', qf, k_past.astype(jnp.float32)) * DA_SCALE
    s_new = jnp.einsum('bqhd,bkhd->bhqk', qf, k_new.astype(jnp.float32)) * DA_SCALE
    s_new = jnp.where(jnp.tril(jnp.ones((T, T), dtype=bool)), s_new, MASK_VALUE)
    a = _da_combine(jnp.concatenate([s_past, s_new], axis=-1), lam)
    return (jnp.einsum('bhqk,bkhd->bqhd', a[..., :P], v_past.astype(jnp.float32))
            + jnp.einsum('bhqk,bkhd->bqhd', a[..., P:], v_new.astype(jnp.float32)))


def _wkv_scan(S0, r, w, k, v, kk, a):
    def step(S, inp):
        r_t, w_t, k_t, v_t, kk_t, a_t = inp
        sa = jnp.einsum('bhij,bhj->bhi', S, -kk_t)
        S = (S * w_t[:, :, None, :] + sa[..., None] * (kk_t * a_t)[:, :, None, :]
             + v_t[..., None] * k_t[:, :, None, :])
        return S, jnp.einsum('bhij,bhj->bhi', S, r_t)

    xs = tuple(jnp.moveaxis(t, 1, 0) for t in (r, w, k, v, kk, a))
    S, o = lax.scan(step, S0, xs)
    return S, jnp.moveaxis(o, 0, 1)


def _rwkv7(rw, shift0, S0, p):
    f32 = jnp.float32
    B, T, _ = rw.shape
    rwf = rw.astype(f32)
    prev = jnp.concatenate([shift0[:, None, :].astype(f32), rwf[:, :-1]], axis=1)
    xm = rwf + (prev - rwf) * p['mu_shift'].astype(f32)
    c1, c2, c3 = RW_WIDTH, 2 * RW_WIDTH, 3 * RW_WIDTH
    r, k, v = xm[..., :c1], xm[..., c1:c2], xm[..., c2:c3]
    dw = xm[..., c3:c3 + W_LORA]
    da = xm[..., c3 + W_LORA:c3 + W_LORA + A_LORA]
    dg = xm[..., c3 + W_LORA + A_LORA:]
    w_log = -jax.nn.softplus(-(p['w0'].astype(f32) + jnp.tanh(dw) @ p['w_up'].astype(f32))) - 0.5
    decay = jnp.exp(-jnp.exp(w_log))
    a = jax.nn.sigmoid(p['a0'].astype(f32) + da @ p['a_up'].astype(f32))
    g = jax.nn.sigmoid(dg) @ p['g_up'].astype(f32)
    heads = lambda t: t.reshape(B, T, N_RW_HEADS, RW_HEAD)
    kk = heads(k * p['k_k'].astype(f32))
    kk = kk * lax.rsqrt(jnp.maximum(jnp.sum(kk * kk, axis=-1, keepdims=True), 1e-24))
    k = k * (1.0 + (a - 1.0) * p['k_a'].astype(f32))
    rh, kh, vh = heads(r), heads(k), heads(v)
    S, o = _wkv_scan(S0, rh, heads(decay), kh, vh, kk, heads(a))
    mean = jnp.mean(o, axis=-1, keepdims=True)
    var = jnp.mean(jnp.square(o - mean), axis=-1, keepdims=True)
    o = ((o - mean) * lax.rsqrt(var + LNX_EPS)).reshape(B, T, RW_WIDTH)
    o = o * p['lnx_w'].astype(f32) + p['lnx_b'].astype(f32)
    bonus = jnp.sum(rh * kh * p['r_k'].astype(f32), axis=-1, keepdims=True) * vh
    o = (o + bonus.reshape(B, T, RW_WIDTH)) * g
    return o, S, rw[:, -1]


def _block(x, pos, S0, shift0, k_past, v_past, lam, lam_init, p):
    B, T, _ = x.shape
    xn = _rmsnorm(x, p['norm_mix'])
    h = jnp.einsum('btd,dn->btn', xn, p['w_in'])
    q = _rmsnorm(h[..., :DA_WIDTH].reshape(B, T, N_DA_SUB, DA_QK_DIM), p['q_gain'])
    k = _rmsnorm(h[..., DA_WIDTH:2 * DA_WIDTH].reshape(B, T, N_DA_SUB, DA_QK_DIM), p['k_gain'])
    q, k = _rope(q, pos), _rope(k, pos)
    v = h[..., 2 * DA_WIDTH:3 * DA_WIDTH].reshape(B, T, N_DA_HEADS, DA_V_DIM)
    if k_past is None:
        o_da = _da_prompt(q, k, v, lam)
    else:
        o_da = _da_sample(q, k, v, k_past, v_past, lam)
    o_da = _rmsnorm(o_da, p['subln_gain']) * (1.0 - lam_init)
    o_rw, S, shift_new = _rwkv7(h[..., 3 * DA_WIDTH:], shift0, S0, p)
    o = jnp.concatenate([o_da.reshape(B, T, DA_WIDTH), o_rw], axis=-1).astype(x.dtype)
    x = x + jnp.einsum('btn,nd->btd', o, p['w_out'])
    xn = _rmsnorm(x, p['norm_ffn'])
    ff = jax.nn.silu(xn @ p['w_gate']) * (xn @ p['w_up_ffn'])
    x = x + ff @ p['w_down']
    return x, k, v, S.astype(x.dtype), shift_new


def setup_inputs(seed: int = 0) -> dict:
    key = jax.random.key(seed)
    keys = jax.random.split(key, 32)
    f32 = jnp.float32
    n_pages = PAST_LEN // PAGE_SIZE
    n_used = DEC_BATCH * n_pages
    n_pool = n_used + (n_used + 3) // 4

    def nrm(i, shape, scale):
        return jax.random.normal(keys[i], shape, f32) * scale

    def gain(i, shape):
        return 1.0 + nrm(i, shape, 0.02)

    page_table = jax.random.permutation(keys[6], n_pool)[:n_used].reshape(DEC_BATCH, n_pages).astype(jnp.int32)
    return {
        'x_prompt': nrm(0, (BATCH, SEQ, D_MODEL), 1.0),
        'x_sample': nrm(1, (DEC_BATCH, DEC_SEQ, D_MODEL), 1.0),
        'cache_k': nrm(2, (DEPTH, n_pool, PAGE_SIZE, N_DA_SUB, DA_QK_DIM), 1.0),
        'cache_v': nrm(3, (DEPTH, n_pool, PAGE_SIZE, N_DA_HEADS, DA_V_DIM), 1.0),
        'state_wkv': nrm(4, (DEPTH, DEC_BATCH, N_RW_HEADS, RW_HEAD, RW_HEAD), 0.1),
        'state_shift': nrm(5, (DEPTH, DEC_BATCH, N_RW_COLS), 1.0),
        'page_table': page_table,
        'norm_mix': gain(7, (DEPTH, D_MODEL)),
        'w_in': nrm(8, (DEPTH, D_MODEL, N_IN), D_MODEL ** -0.5),
        'q_gain': gain(9, (DEPTH, DA_QK_DIM)),
        'k_gain': gain(10, (DEPTH, DA_QK_DIM)),
        'lambda_q1': nrm(11, (DEPTH, DA_QK_DIM), 0.1),
        'lambda_k1': nrm(12, (DEPTH, DA_QK_DIM), 0.1),
        'lambda_q2': nrm(13, (DEPTH, DA_QK_DIM), 0.1),
        'lambda_k2': nrm(14, (DEPTH, DA_QK_DIM), 0.1),
        'subln_gain': gain(15, (DEPTH, DA_V_DIM)),
        'mu_shift': jax.random.uniform(keys[16], (DEPTH, N_RW_COLS), f32),
        'w0': jax.random.uniform(keys[17], (DEPTH, RW_WIDTH), f32, -6.5, -1.5),
        'w_up': nrm(18, (DEPTH, W_LORA, RW_WIDTH), 0.1 * W_LORA ** -0.5),
        'a0': nrm(19, (DEPTH, RW_WIDTH), 0.1),
        'a_up': nrm(20, (DEPTH, A_LORA, RW_WIDTH), A_LORA ** -0.5),
        'g_up': nrm(21, (DEPTH, G_LORA, RW_WIDTH), G_LORA ** -0.5),
        'k_k': 0.85 + nrm(22, (DEPTH, RW_WIDTH), 0.05),
        'k_a': 1.0 + nrm(23, (DEPTH, RW_WIDTH), 0.05),
        'r_k': nrm(24, (DEPTH, N_RW_HEADS, RW_HEAD), 0.1),
        'lnx_w': gain(25, (DEPTH, RW_WIDTH)),
        'lnx_b': nrm(26, (DEPTH, RW_WIDTH), 0.02),
        'w_out': nrm(27, (DEPTH, D_MODEL, D_MODEL), D_MODEL ** -0.5),
        'norm_ffn': gain(28, (DEPTH, D_MODEL)),
        'w_gate': nrm(29, (DEPTH, D_MODEL, D_FF), D_MODEL ** -0.5),
        'w_up_ffn': nrm(30, (DEPTH, D_MODEL, D_FF), D_MODEL ** -0.5),
        'w_down': nrm(31, (DEPTH, D_FF, D_MODEL), D_FF ** -0.5),
    }


def reference(x_prompt, x_sample, cache_k, cache_v, state_wkv, state_shift, page_table,
              norm_mix, w_in, q_gain, k_gain, lambda_q1, lambda_k1, lambda_q2, lambda_k2,
              subln_gain, mu_shift, w0, w_up, a0, a_up, g_up, k_k, k_a, r_k, lnx_w, lnx_b,
              w_out, norm_ffn, w_gate, w_up_ffn, w_down):
    f32 = jnp.float32
    B, S, _ = x_prompt.shape
    Bd, T, _ = x_sample.shape
    n_pages = page_table.shape[1]
    past_len = n_pages * PAGE_SIZE
    pos_p = jnp.arange(S, dtype=jnp.int32)
    pos_s = past_len + jnp.arange(T, dtype=jnp.int32)
    xp, xs = x_prompt, x_sample
    k_p_rows, v_p_rows, wkv_p, shift_p = [], [], [], []
    k_s_rows, v_s_rows, wkv_s, shift_s = [], [], [], []
    for l in range(DEPTH):
        lam_init = 0.8 - 0.6 * math.exp(-0.3 * l)
        lam = (jnp.exp(jnp.sum(lambda_q1[l].astype(f32) * lambda_k1[l].astype(f32)))
               - jnp.exp(jnp.sum(lambda_q2[l].astype(f32) * lambda_k2[l].astype(f32)))
               + lam_init)
        p = {
            'norm_mix': norm_mix[l], 'w_in': w_in[l], 'q_gain': q_gain[l], 'k_gain': k_gain[l],
            'subln_gain': subln_gain[l], 'mu_shift': mu_shift[l], 'w0': w0[l], 'w_up': w_up[l],
            'a0': a0[l], 'a_up': a_up[l], 'g_up': g_up[l], 'k_k': k_k[l], 'k_a': k_a[l],
            'r_k': r_k[l], 'lnx_w': lnx_w[l], 'lnx_b': lnx_b[l], 'w_out': w_out[l],
            'norm_ffn': norm_ffn[l], 'w_gate': w_gate[l], 'w_up_ffn': w_up_ffn[l], 'w_down': w_down[l],
        }
        S0_p = jnp.zeros((B, N_RW_HEADS, RW_HEAD, RW_HEAD), f32)
        sh0_p = jnp.zeros((B, N_RW_COLS), xp.dtype)
        xp, k_new, v_new, S_new, sh_new = _block(xp, pos_p, S0_p, sh0_p, None, None, lam, lam_init, p)
        k_p_rows.append(k_new.reshape(B, S // PAGE_SIZE, PAGE_SIZE, N_DA_SUB, DA_QK_DIM))
        v_p_rows.append(v_new.reshape(B, S // PAGE_SIZE, PAGE_SIZE, N_DA_HEADS, DA_V_DIM))
        wkv_p.append(S_new)
        shift_p.append(sh_new)
        k_past = cache_k[l, page_table].reshape(Bd, past_len, N_DA_SUB, DA_QK_DIM)
        v_past = cache_v[l, page_table].reshape(Bd, past_len, N_DA_HEADS, DA_V_DIM)
        xs, k_new, v_new, S_new, sh_new = _block(xs, pos_s, state_wkv[l].astype(f32), state_shift[l],
                                                 k_past, v_past, lam, lam_init, p)
        k_s_rows.append(k_new)
        v_s_rows.append(v_new)
        wkv_s.append(S_new)
        shift_s.append(sh_new)
    return (xp, xs, jnp.stack(k_p_rows), jnp.stack(v_p_rows), jnp.stack(wkv_p), jnp.stack(shift_p),
            jnp.stack(k_s_rows), jnp.stack(v_s_rows), jnp.stack(wkv_s), jnp.stack(shift_s))
```

```python
import functools
import math

import jax
import jax.numpy as jnp
from jax import lax
from jax.experimental import pallas as pl
from jax.experimental.pallas import tpu as pltpu

F32 = jnp.float32
BF16 = jnp.bfloat16
HIGHEST = lax.Precision.HIGHEST

D_MODEL = 2048
PAGE_SIZE = 128
DA_WIDTH = 1024
RW_WIDTH = 1024
DA_V_DIM = 128
N_DA_HEADS = 8
DA_QK_DIM = 64
N_DA_SUB = 16
RW_HEAD = 64
N_RW_HEADS = 16
N_RW_PAIRS = N_RW_HEADS // 2
W_LORA = 64
A_LORA = 64
G_LORA = 160
N_RW_COLS = 3 * RW_WIDTH + W_LORA + A_LORA + G_LORA
LANES = 128
N_RW_PAD = ((N_RW_COLS + LANES - 1) // LANES) * LANES
G_PAD = N_RW_PAD - 3 * RW_WIDTH - LANES
D_FF = 5632
ROPE_THETA = 10000.0
NORM_EPS = 1e-6
LNX_EPS = 64e-5
MASK_VALUE = -1e30
DA_SCALE = DA_QK_DIM ** -0.5
VMEM_LIMIT_BYTES = 56 * 1024 * 1024
NEW_PAD = 16


def _cp(*sem):
    return pltpu.CompilerParams(dimension_semantics=sem, vmem_limit_bytes=VMEM_LIMIT_BYTES)


def _dot(a, b, precision=None):
    return jnp.dot(a, b, preferred_element_type=F32, precision=precision)


def _dot_nt(a, b, precision=None):
    return lax.dot_general(a, b, (((1,), (1,)), ((), ())), preferred_element_type=F32,
                           precision=precision)


def _dot_tn(a, b, precision=None):
    return lax.dot_general(a, b, (((0,), (0,)), ((), ())), preferred_element_type=F32,
                           precision=precision)


def _seg_ones(n, seg):
    r = lax.broadcasted_iota(jnp.int32, (n, n), 0) // seg
    c = lax.broadcasted_iota(jnp.int32, (n, n), 1) // seg
    return (r == c).astype(F32)


def _norm_mm_kernel(x_ref, g_ref, w_ref, o_ref, xn_ref):
    @pl.when(pl.program_id(1) == 0)
    def _():
        x = x_ref[...]
        ms = jnp.mean(x * x, axis=-1, keepdims=True)
        xn_ref[...] = (x * lax.rsqrt(ms + NORM_EPS) * g_ref[...]).astype(BF16)

    o_ref[...] = _dot(xn_ref[...], w_ref[...])


def _norm_mm(x, g, w, tn):
    M, D = x.shape
    N = w.shape[1]
    tm = min(M, 1024)
    return pl.pallas_call(
        _norm_mm_kernel,
        grid=(M // tm, N // tn),
        in_specs=[pl.BlockSpec((tm, D), lambda i, j: (i, 0)),
                  pl.BlockSpec((1, D), lambda i, j: (0, 0)),
                  pl.BlockSpec((D, tn), lambda i, j: (0, j))],
        out_specs=pl.BlockSpec((tm, tn), lambda i, j: (i, j)),
        out_shape=jax.ShapeDtypeStruct((M, N), F32),
        scratch_shapes=[pltpu.VMEM((tm, D), BF16)],
        compiler_params=_cp("parallel", "arbitrary"),
        name="norm_in_proj",
    )(x, g, w)


def _qk_post_kernel(hq_ref, hk_ref, qg_ref, kg_ref, cos_ref, sin_ref, qb_ref, kf_ref, kb_ref):
    tm = hq_ref.shape[0]
    lane = lax.broadcasted_iota(jnp.int32, (tm, LANES), 1)
    first = (lane % DA_QK_DIM) < (DA_QK_DIM // 2)
    seg = _seg_ones(LANES, DA_QK_DIM) * (1.0 / DA_QK_DIM)
    cos = cos_ref[...]
    sin = sin_ref[...]

    def proc(x, g):
        ms = _dot(x * x, seg, HIGHEST)
        y = x * lax.rsqrt(ms + NORM_EPS) * g
        partner = jnp.where(first, pltpu.roll(y, LANES - DA_QK_DIM // 2, 1),
                            pltpu.roll(y, DA_QK_DIM // 2, 1))
        return y * cos + partner * sin

    for c in range(DA_WIDTH // LANES):
        sl = slice(LANES * c, LANES * (c + 1))
        q = proc(hq_ref[:, sl], qg_ref[...])
        qb_ref[:, sl] = (q * DA_SCALE).astype(BF16)
        k = proc(hk_ref[:, sl], kg_ref[...])
        kf_ref[:, sl] = k
        kb_ref[:, sl] = k.astype(BF16)


def _qk_post(hda, qg, kg, cos, sin, T):
    M = hda.shape[0]
    tm = min(M, 512)
    if cos.shape[0] == M:
        tab_map = lambda i: (i, 0)
    else:
        nper = T // tm
        tab_map = lambda i: (i % nper, 0)
    blk = lambda c: pl.BlockSpec((tm, DA_WIDTH), lambda i: (i, c))
    vec = pl.BlockSpec((1, LANES), lambda i: (0, 0))
    tab = pl.BlockSpec((tm, LANES), tab_map)
    return pl.pallas_call(
        _qk_post_kernel,
        grid=(M // tm,),
        in_specs=[blk(0), blk(1), vec, vec, tab, tab],
        out_specs=[blk(0), blk(0), blk(0)],
        out_shape=[jax.ShapeDtypeStruct((M, DA_WIDTH), BF16),
                   jax.ShapeDtypeStruct((M, DA_WIDTH), F32),
                   jax.ShapeDtypeStruct((M, DA_WIDTH), BF16)],
        compiler_params=_cp("parallel"),
        name="qk_norm_rope",
    )(hda, hda, qg, kg, cos, sin)


def _subln(o, g, lam_init):
    ms = jnp.mean(o * o, axis=-1, keepdims=True)
    return o * lax.rsqrt(ms + NORM_EPS) * g * (1.0 - lam_init)


def _attn_kernel(lam_ref, q_ref, k_ref, v_ref, g_ref, o_ref, *, tq, tk, lam_init):
    qi = pl.program_id(2)
    q = q_ref[0]
    lane = lax.broadcasted_iota(jnp.int32, (tq, LANES), 1)
    zero = jnp.zeros_like(q)
    q1 = jnp.where(lane < DA_QK_DIM, q, zero)
    q2 = jnp.where(lane >= DA_QK_DIM, q, zero)
    row = qi * tq + lax.broadcasted_iota(jnp.int32, (tq, tk), 0)
    col0 = lax.broadcasted_iota(jnp.int32, (tq, tk), 1)

    def body(j, carry):
        m1, l1, a1, m2, l2, a2 = carry
        off = pl.multiple_of(j * tk, tk)
        kb = k_ref[0, pl.ds(off, tk), :]
        vb = v_ref[0, pl.ds(off, tk), :].astype(BF16)
        msk = (col0 + j * tk) <= row

        def upd(qx, m, l, a):
            s = _dot_nt(qx, kb)
            s = jnp.where(msk, s, MASK_VALUE)
            mn = jnp.maximum(m, jnp.max(s, axis=-1, keepdims=True))
            al = jnp.exp(m - mn)
            p = jnp.exp(s - mn)
            l = l * al + jnp.sum(p, axis=-1, keepdims=True)
            a = a * al + _dot(p.astype(BF16), vb)
            return mn, l, a

        m1, l1, a1 = upd(q1, m1, l1, a1)
        m2, l2, a2 = upd(q2, m2, l2, a2)
        return m1, l1, a1, m2, l2, a2

    m0 = jnp.full((tq, 1), MASK_VALUE, F32)
    l0 = jnp.zeros((tq, 1), F32)
    a0 = jnp.zeros((tq, DA_V_DIM), F32)
    nkv = (qi * tq + tq + tk - 1) // tk
    m1, l1, a1, m2, l2, a2 = lax.fori_loop(0, nkv, body, (m0, l0, a0, m0, l0, a0))
    o = a1 / l1 - lam_ref[0] * (a2 / l2)
    o_ref[0] = _subln(o, g_ref[...], lam_init).astype(o_ref.dtype)


def _attn_prompt(lam, qb, kb, hda3, g, lam_init):
    B, S, _ = qb.shape
    tq = tk = min(S, 256)
    v_blk0 = 2 * DA_WIDTH // DA_V_DIM
    return pl.pallas_call(
        functools.partial(_attn_kernel, tq=tq, tk=tk, lam_init=lam_init),
        grid=(B, N_DA_HEADS, S // tq),
        in_specs=[pl.BlockSpec(memory_space=pltpu.SMEM),
                  pl.BlockSpec((1, tq, LANES), lambda b, h, i: (b, i, h)),
                  pl.BlockSpec((1, S, LANES), lambda b, h, i: (b, 0, h)),
                  pl.BlockSpec((1, S, DA_V_DIM), lambda b, h, i: (b, 0, v_blk0 + h)),
                  pl.BlockSpec((1, DA_V_DIM), lambda b, h, i: (0, 0))],
        out_specs=pl.BlockSpec((1, tq, DA_V_DIM), lambda b, h, i: (b, i, h)),
        out_shape=jax.ShapeDtypeStruct((B, S, DA_WIDTH), BF16),
        compiler_params=_cp("parallel", "parallel", "parallel"),
        name="diff_attn_prompt",
    )(lam, qb, kb, hda3, g)


def _dec_attn_kernel(pt_ref, lam_ref, q_ref, kn_ref, vn_ref, g_ref, *rest, pp, n_new, lam_init):
    k_refs = rest[:pp]
    v_refs = rest[pp:2 * pp]
    o_ref = rest[2 * pp]
    m_scr, l_scr, acc_scr = rest[2 * pp + 1:]
    p_id = pl.program_id(1)
    nrow = 2 * n_new * N_DA_HEADS

    @pl.when(p_id == 0)
    def _():
        m_scr[...] = jnp.full(m_scr.shape, MASK_VALUE, F32)
        l_scr[...] = jnp.zeros(l_scr.shape, F32)
        acc_scr[...] = jnp.zeros(acc_scr.shape, F32)

    q = q_ref[0]

    def update(s, vs):
        m = m_scr[...]
        mn = jnp.maximum(m, jnp.max(s, axis=-1, keepdims=True))
        al = jnp.exp(m - mn)
        p = jnp.exp(s - mn)
        l_scr[...] = l_scr[...] * al + jnp.sum(p, axis=-1, keepdims=True)
        acc = acc_scr[...] * al
        w = s.shape[1] // len(vs)
        for i, vb in enumerate(vs):
            acc = acc + _dot(p[:, i * w:(i + 1) * w].astype(BF16), vb)
        acc_scr[...] = acc
        m_scr[...] = mn

    s_all = jnp.concatenate([_dot_nt(q, k_refs[i][0, 0].astype(BF16)) for i in range(pp)], axis=1)
    update(s_all, [v_refs[i][0, 0].astype(BF16) for i in range(pp)])

    @pl.when(p_id == pl.num_programs(1) - 1)
    def _():
        s = _dot_nt(q, kn_ref[0].astype(BF16))
        r = lax.broadcasted_iota(jnp.int32, s.shape, 0)
        t = lax.broadcasted_iota(jnp.int32, s.shape, 1)
        qq = (r % (n_new * N_DA_HEADS)) // N_DA_HEADS
        s = jnp.where(t <= qq, s, MASK_VALUE)
        update(s, [vn_ref[0].astype(BF16)])
        n = acc_scr[...] / l_scr[...]
        half = nrow // 2
        comb = n[:half] - lam_ref[0] * n[half:]
        rr = lax.broadcasted_iota(jnp.int32, comb.shape, 0)
        cc = lax.broadcasted_iota(jnp.int32, comb.shape, 1)
        comb = jnp.where((rr % N_DA_HEADS) == (cc // DA_V_DIM), comb, 0.0)
        sr = lax.broadcasted_iota(jnp.int32, (8, half), 0)
        sc = lax.broadcasted_iota(jnp.int32, (8, half), 1)
        sel = (sc // N_DA_HEADS == sr).astype(F32)
        out = _dot(sel, comb, HIGHEST)
        for c in range(N_DA_HEADS):
            sl = slice(DA_V_DIM * c, DA_V_DIM * (c + 1))
            o_ref[0, :, sl] = _subln(out[:n_new, sl], g_ref[...], lam_init)


def _attn_sample(page_table, lam, qrows, k_new, v_new, g, cache_k4, cache_v4, layer, lam_init, pp=4):
    Bd, nrow, _ = qrows.shape
    n_new = nrow // (2 * N_DA_HEADS)
    n_pages = page_table.shape[1]
    pt = page_table.reshape(-1)

    def page_spec(i):
        return pl.BlockSpec((1, 1, PAGE_SIZE, DA_WIDTH),
                            lambda b, p, pt_ref: (layer, pt_ref[b * n_pages + p * pp + i], 0, 0))

    row_spec = lambda r: pl.BlockSpec((1, r, DA_WIDTH), lambda b, p, pt_ref: (b, 0, 0))
    grid_spec = pltpu.PrefetchScalarGridSpec(
        num_scalar_prefetch=1,
        grid=(Bd, n_pages // pp),
        in_specs=[pl.BlockSpec(memory_space=pltpu.SMEM),
                  row_spec(nrow), row_spec(NEW_PAD), row_spec(NEW_PAD),
                  pl.BlockSpec((1, DA_V_DIM), lambda b, p, pt_ref: (0, 0))]
                 + [page_spec(i) for i in range(pp)] + [page_spec(i) for i in range(pp)],
        out_specs=row_spec(n_new),
        scratch_shapes=[pltpu.VMEM((nrow, 1), F32), pltpu.VMEM((nrow, 1), F32),
                        pltpu.VMEM((nrow, DA_WIDTH), F32)],
    )
    return pl.pallas_call(
        functools.partial(_dec_attn_kernel, pp=pp, n_new=n_new, lam_init=lam_init),
        grid_spec=grid_spec,
        out_shape=jax.ShapeDtypeStruct((Bd, n_new, DA_WIDTH), F32),
        compiler_params=_cp("parallel", "arbitrary"),
        name="diff_attn_decode",
    )(pt, lam, qrows, k_new, v_new, g, *([cache_k4] * pp), *([cache_v4] * pp))


def _rw_prep_kernel(h_ref, sh0_ref, mu_ref, w0_ref, a0_ref, wup_ref, aup_ref, gup_ref,
                    r_ref, k_ref, v_ref, lw_ref, a_ref, g_ref, prev_scr):
    cp = h_ref.shape[1]

    @pl.when(pl.program_id(1) == 0)
    def _():
        prev_scr[...] = sh0_ref[0]

    rw = h_ref[0]
    rowi = lax.broadcasted_iota(jnp.int32, rw.shape, 0)
    prev = jnp.where(rowi == 0, prev_scr[...], pltpu.roll(rw, 1, 0))
    prev_scr[...] = rw[cp - 1:cp, :]
    xm = rw + (prev - rw) * mu_ref[...]
    c1, c2, c3 = RW_WIDTH, 2 * RW_WIDTH, 3 * RW_WIDTH
    r_ref[0] = xm[:, :c1]
    k_ref[0] = xm[:, c1:c2]
    v_ref[0] = xm[:, c2:c3]
    lo = xm[:, c3:c3 + LANES]
    dg = xm[:, c3 + LANES:]
    z = -(w0_ref[...] + _dot(jnp.tanh(lo), wup_ref[...], HIGHEST))
    softplus = jnp.maximum(z, 0.0) + jnp.log1p(jnp.exp(-jnp.abs(z)))
    lw_ref[0] = -jnp.exp(-softplus - 0.5)
    a_ref[0] = jax.nn.sigmoid(a0_ref[...] + _dot(lo, aup_ref[...], HIGHEST))
    g_ref[0] = _dot(jax.nn.sigmoid(dg), gup_ref[...], HIGHEST)


def _rw_prep(hrw3, sh0, mu, w0, a0, wup, aup, gup):
    B, T, _ = hrw3.shape
    cp = min(T, 256)
    full = lambda shape: pl.BlockSpec(shape, lambda b, c: (0,) * len(shape))
    out_blk = pl.BlockSpec((1, cp, RW_WIDTH), lambda b, c: (b, c, 0))
    out_sds = jax.ShapeDtypeStruct((B, T, RW_WIDTH), F32)
    return pl.pallas_call(
        _rw_prep_kernel,
        grid=(B, T // cp),
        in_specs=[pl.BlockSpec((1, cp, N_RW_PAD), lambda b, c: (b, c, 0)),
                  pl.BlockSpec((1, 1, N_RW_PAD), lambda b, c: (b, 0, 0)),
                  full((1, N_RW_PAD)), full((1, RW_WIDTH)), full((1, RW_WIDTH)),
                  full((LANES, RW_WIDTH)), full((LANES, RW_WIDTH)), full((G_PAD, RW_WIDTH))],
        out_specs=[out_blk] * 6,
        out_shape=[out_sds] * 6,
        scratch_shapes=[pltpu.VMEM((1, N_RW_PAD), F32)],
        compiler_params=_cp("parallel", "arbitrary"),
        name="rwkv_prep",
    )(hrw3, sh0, mu, w0, a0, wup, aup, gup)


def _rw_scan_kernel(r_ref, k_ref, v_ref, lw_ref, a_ref, g_ref, kk_ref, ka_ref, rk_ref,
                    lnw_ref, lnb_ref, h0_ref, o_ref, hout_ref, h_scr, *, t_valid):
    C = r_ref.shape[1]
    ci = pl.program_id(2)

    @pl.when(ci == 0)
    def _():
        h_scr[...] = h0_ref[0, 0]

    r = r_ref[0]
    k = k_ref[0]
    v = v_ref[0]
    lw = lw_ref[0]
    a = a_ref[0]
    if t_valid < C:
        valid = lax.broadcasted_iota(jnp.int32, (C, LANES), 0) < t_valid
        lw = jnp.where(valid, lw, 0.0)
        k = jnp.where(valid, k, 0.0)
        v = jnp.where(valid, v, 0.0)
    seg = _seg_ones(LANES, RW_HEAD)
    kk = k * kk_ref[...]
    kk = kk * lax.rsqrt(jnp.maximum(_dot(kk * kk, seg, HIGHEST), 1e-24))
    k2 = k * (1.0 + (a - 1.0) * ka_ref[...])
    b = kk * a

    ti = lax.broadcasted_iota(jnp.int32, (C, C), 0)
    si = lax.broadcasted_iota(jnp.int32, (C, C), 1)
    incl = si <= ti
    strict = si < ti
    eye = (si == ti).astype(F32)
    gam = _dot(incl.astype(F32), lw, HIGHEST)
    gl = gam[C - 1:C, :]
    rt = r * jnp.exp(gam)
    at = -kk * jnp.exp(gam - lw)
    ginv = jnp.exp(-gam)
    bt = b * ginv
    kt = k2 * ginv
    gh = jnp.exp(gl - gam)
    bh = b * gh
    kh = k2 * gh

    H = h_scr[...]
    X = _dot(at, H, HIGHEST)
    RH = _dot(rt, H, HIGHEST)
    lane = lax.broadcasted_iota(jnp.int32, (C, LANES), 1)
    head_a = lane < RW_HEAD
    n_dbl = max(int(math.ceil(math.log2(C))) - 1, 0)
    us, ds = [], []
    for hd in range(2):
        mh = head_a if hd == 0 else jnp.logical_not(head_a)
        atm = jnp.where(mh, at, 0.0)
        rtm = jnp.where(mh, rt, 0.0)
        aab = jnp.where(strict, _dot_nt(atm, bt, HIGHEST), 0.0)
        aak = jnp.where(strict, _dot_nt(atm, kt, HIGHEST), 0.0)
        arb = jnp.where(incl, _dot_nt(rtm, bt, HIGHEST), 0.0)
        ark = jnp.where(incl, _dot_nt(rtm, kt, HIGHEST), 0.0)
        npow = aab
        tinv = eye + aab
        for _ in range(n_dbl):
            npow = _dot(npow, npow, HIGHEST)
            tinv = tinv + _dot(npow, tinv, HIGHEST)
        y = _dot(aak, v, HIGHEST)
        u = _dot(tinv, X + y, HIGHEST)
        us.append(u)
        ds.append(_dot(arb, u, HIGHEST) + _dot(ark, v, HIGHEST))
    U = jnp.where(head_a, us[0], us[1])
    O = RH + jnp.where(head_a, ds[0], ds[1])

    hr = lax.broadcasted_iota(jnp.int32, (LANES, LANES), 0)
    hc = lax.broadcasted_iota(jnp.int32, (LANES, LANES), 1)
    dg = jnp.where(hr == hc, jnp.exp(gl), 0.0)
    lhs = jnp.concatenate([bh, kh, dg], axis=0)
    rhs = jnp.concatenate([U, v, H], axis=0)
    hn = _dot_tn(lhs, rhs, HIGHEST)
    hn = jnp.where((hr // RW_HEAD) == (hc // RW_HEAD), hn, 0.0)
    h_scr[...] = hn

    @pl.when(ci == pl.num_programs(2) - 1)
    def _():
        hout_ref[0, 0] = hn

    segm = seg * (1.0 / RW_HEAD)
    mean = _dot(O, segm, HIGHEST)
    d = O - mean
    var = _dot(d * d, segm, HIGHEST)
    on = d * lax.rsqrt(var + LNX_EPS) * lnw_ref[...] + lnb_ref[...]
    bonus = _dot(r * k2 * rk_ref[...], seg, HIGHEST) * v
    o_ref[0] = ((on + bonus) * g_ref[0]).astype(o_ref.dtype)


def _rw_scan(r, k, v, lw, a, g, kk, ka, rk, lnw, lnb, h0, t_valid, out_dtype):
    B, T, _ = r.shape
    C = min(T, 128)
    tok = pl.BlockSpec((1, C, LANES), lambda b, p, c: (b, c, p))
    vec = pl.BlockSpec((1, LANES), lambda b, p, c: (0, p))
    st = pl.BlockSpec((1, 1, LANES, LANES), lambda b, p, c: (b, p, 0, 0))
    return pl.pallas_call(
        functools.partial(_rw_scan_kernel, t_valid=t_valid),
        grid=(B, N_RW_PAIRS, T // C),
        in_specs=[tok] * 6 + [vec] * 5 + [st],
        out_specs=[tok, st],
        out_shape=[jax.ShapeDtypeStruct((B, T, RW_WIDTH), out_dtype),
                   jax.ShapeDtypeStruct((B, N_RW_PAIRS, LANES, LANES), F32)],
        scratch_shapes=[pltpu.VMEM((LANES, LANES), F32)],
        compiler_params=_cp("parallel", "parallel", "arbitrary"),
        name="rwkv_scan",
    )(r, k, v, lw, a, g, kk, ka, rk, lnw, lnb, h0)


def _out_proj_kernel(a1_ref, a2_ref, w_ref, res_ref, o_ref):
    acc = _dot(a1_ref[...].astype(BF16), w_ref[:DA_WIDTH, :])
    acc = acc + _dot(a2_ref[...].astype(BF16), w_ref[DA_WIDTH:, :])
    o_ref[...] = res_ref[...] + acc


def _out_proj(a1, a2, w, res):
    M, D = res.shape
    tm = min(M, 1024)
    tn = 512
    return pl.pallas_call(
        _out_proj_kernel,
        grid=(M // tm, D // tn),
        in_specs=[pl.BlockSpec((tm, DA_WIDTH), lambda i, j: (i, 0)),
                  pl.BlockSpec((tm, RW_WIDTH), lambda i, j: (i, 0)),
                  pl.BlockSpec((DA_WIDTH + RW_WIDTH, tn), lambda i, j: (0, j)),
                  pl.BlockSpec((tm, tn), lambda i, j: (i, j))],
        out_specs=pl.BlockSpec((tm, tn), lambda i, j: (i, j)),
        out_shape=jax.ShapeDtypeStruct((M, D), F32),
        compiler_params=_cp("parallel", "parallel"),
        name="out_proj",
    )(a1, a2, w, res)


def _ffn_up_kernel(x_ref, g_ref, wg_ref, wu_ref, o_ref, xn_ref):
    @pl.when(pl.program_id(1) == 0)
    def _():
        x = x_ref[...]
        ms = jnp.mean(x * x, axis=-1, keepdims=True)
        xn_ref[...] = (x * lax.rsqrt(ms + NORM_EPS) * g_ref[...]).astype(BF16)

    xn = xn_ref[...]
    gate = _dot(xn, wg_ref[...])
    up = _dot(xn, wu_ref[...])
    o_ref[...] = (gate * jax.nn.sigmoid(gate) * up).astype(o_ref.dtype)


def _ffn_up(x, g, wg, wu):
    M, D = x.shape
    tm = min(M, 1024)
    tn = 512
    return pl.pallas_call(
        _ffn_up_kernel,
        grid=(M // tm, D_FF // tn),
        in_specs=[pl.BlockSpec((tm, D), lambda i, j: (i, 0)),
                  pl.BlockSpec((1, D), lambda i, j: (0, 0)),
                  pl.BlockSpec((D, tn), lambda i, j: (0, j)),
                  pl.BlockSpec((D, tn), lambda i, j: (0, j))],
        out_specs=pl.BlockSpec((tm, tn), lambda i, j: (i, j)),
        out_shape=jax.ShapeDtypeStruct((M, D_FF), BF16),
        scratch_shapes=[pltpu.VMEM((tm, D), BF16)],
        compiler_params=_cp("parallel", "arbitrary"),
        name="ffn_gate_up",
    )(x, g, wg, wu)


def _ffn_down_kernel(a_ref, w_ref, res_ref, o_ref):
    o_ref[...] = res_ref[...] + _dot(a_ref[...], w_ref[...])


def _ffn_down(a, w, res):
    M, D = res.shape
    tm = min(M, 1024)
    tn = 512
    return pl.pallas_call(
        _ffn_down_kernel,
        grid=(M // tm, D // tn),
        in_specs=[pl.BlockSpec((tm, D_FF), lambda i, j: (i, 0)),
                  pl.BlockSpec((D_FF, tn), lambda i, j: (0, j)),
                  pl.BlockSpec((tm, tn), lambda i, j: (i, j))],
        out_specs=pl.BlockSpec((tm, tn), lambda i, j: (i, j)),
        out_shape=jax.ShapeDtypeStruct((M, D), F32),
        compiler_params=_cp("parallel", "parallel"),
        name="ffn_down",
    )(a, w, res)


def _rope_tables(pos):
    half = DA_QK_DIM // 2
    inv_freq = jnp.power(ROPE_THETA, -jnp.arange(half, dtype=F32) / half)
    ang = pos.astype(F32)[:, None] * inv_freq[None, :]
    cos = jnp.cos(ang)
    sin = jnp.sin(ang)
    reps = LANES // DA_QK_DIM
    cos_t = jnp.tile(jnp.concatenate([cos, cos], axis=-1), (1, reps))
    sin_t = jnp.tile(jnp.concatenate([-sin, sin], axis=-1), (1, reps))
    return cos_t, sin_t


def _state_to_pairs(S):
    B = S.shape[0]
    St = jnp.swapaxes(S, -1, -2).reshape(B, N_RW_PAIRS, 2, RW_HEAD, RW_HEAD)
    eye2 = jnp.eye(2, dtype=S.dtype)
    Hp = St[:, :, :, :, None, :] * eye2[None, None, :, None, :, None]
    return Hp.reshape(B, N_RW_PAIRS, LANES, LANES)


def _pairs_to_state(Hp):
    B = Hp.shape[0]
    H5 = Hp.reshape(B, N_RW_PAIRS, 2, RW_HEAD, 2, RW_HEAD)
    diag = jnp.stack([H5[:, :, 0, :, 0, :], H5[:, :, 1, :, 1, :]], axis=2)
    return jnp.swapaxes(diag.reshape(B, N_RW_HEADS, RW_HEAD, RW_HEAD), -1, -2)


def _decode_query_rows(qb):
    Bd, n_new, _ = qb.shape
    sub = jnp.arange(DA_WIDTH) // DA_QK_DIM
    want = (2 * jnp.arange(N_DA_HEADS)[None, :] + jnp.arange(2)[:, None])
    mask = (sub[None, None, :] == want[:, :, None])
    rows = jnp.where(mask[None, :, None, :, :], qb[:, None, :, None, :], jnp.zeros((), qb.dtype))
    return rows.reshape(Bd, 2 * n_new * N_DA_HEADS, DA_WIDTH)


def _pad_rows(x, n):
    return jnp.pad(x, ((0, 0), (0, n - x.shape[1]), (0, 0)))


def _layer(x3, cos, sin, S0, shift0, lam, lam_init, p, decode=None):
    B, T, D = x3.shape
    M = B * T
    x = x3.reshape(M, D)
    hda = _norm_mm(x, p['norm_mix'], p['w_da'], 1024)
    hrw = _norm_mm(x, p['norm_mix'], p['w_rw'], 1152)
    qb, kf, kb = _qk_post(hda, p['q_gain'], p['k_gain'], cos, sin, T)
    hda3 = hda.reshape(B, T, 3 * DA_WIDTH)
    v_new = hda3[:, :, 2 * DA_WIDTH:]
    if decode is None:
        o_da = _attn_prompt(lam, qb.reshape(B, T, DA_WIDTH), kb.reshape(B, T, DA_WIDTH), hda3,
                            p['subln_gain'], lam_init)
    else:
        page_table, cache_k4, cache_v4, layer = decode
        qrows = _decode_query_rows(qb.reshape(B, T, DA_WIDTH))
        o_da = _attn_sample(page_table, lam, qrows, _pad_rows(kf.reshape(B, T, DA_WIDTH), NEW_PAD),
                            _pad_rows(v_new, NEW_PAD), p['subln_gain'], cache_k4, cache_v4,
                            layer, lam_init)
    hrw3 = hrw.reshape(B, T, N_RW_PAD)
    shift_new = hrw3[:, -1, :N_RW_COLS]
    t_pad = max(T, 8)
    hrw3p = _pad_rows(hrw3, t_pad)
    sh0 = jnp.pad(shift0, ((0, 0), (0, N_RW_PAD - N_RW_COLS)))[:, None, :]
    r, k, v, lw, a, g = _rw_prep(hrw3p, sh0, p['mu_shift'], p['w0'], p['a0'], p['w_up'], p['a_up'],
                                 p['g_up'])
    o_rw, hp = _rw_scan(r, k, v, lw, a, g, p['k_k'], p['k_a'], p['r_k'], p['lnx_w'], p['lnx_b'],
                        _state_to_pairs(S0), T, BF16 if decode is None else F32)
    o_rw = o_rw[:, :T]
    S_new = _pairs_to_state(hp)
    x1 = _out_proj(o_da.reshape(M, DA_WIDTH), o_rw.reshape(M, RW_WIDTH), p['w_out'], x)
    ff = _ffn_up(x1, p['norm_ffn'], p['w_gate'], p['w_up_ffn'])
    x2 = _ffn_down(ff, p['w_down'], x1)
    return x2.reshape(B, T, D), kf, v_new, S_new, shift_new


def kernel(x_prompt, x_sample, cache_k, cache_v, state_wkv, state_shift, page_table, norm_mix, w_in, q_gain, k_gain, lambda_q1, lambda_k1, lambda_q2, lambda_k2, subln_gain, mu_shift, w0, w_up, a0, a_up, g_up, k_k, k_a, r_k, lnx_w, lnx_b, w_out, norm_ffn, w_gate, w_up_ffn, w_down):
    B, S, _ = x_prompt.shape
    Bd, T, _ = x_sample.shape
    depth = w_in.shape[0]
    n_pages = page_table.shape[1]
    past_len = n_pages * PAGE_SIZE
    cos_p, sin_p = _rope_tables(jnp.arange(S, dtype=jnp.int32))
    cos_s, sin_s = _rope_tables(past_len + jnp.arange(T, dtype=jnp.int32))
    cos_s = jnp.tile(cos_s, (Bd, 1))
    sin_s = jnp.tile(sin_s, (Bd, 1))
    n_pool = cache_k.shape[1]
    cache_k4 = cache_k.reshape(depth, n_pool, PAGE_SIZE, DA_WIDTH)
    cache_v4 = cache_v.reshape(depth, n_pool, PAGE_SIZE, DA_WIDTH)
    row = lambda t: t.reshape(1, -1)
    tile_gain = lambda t: jnp.tile(t, LANES // DA_QK_DIM).reshape(1, LANES)

    xp, xs = x_prompt, x_sample
    outs = [[] for _ in range(8)]
    for l in range(depth):
        lam_init = 0.8 - 0.6 * math.exp(-0.3 * l)
        lam = (jnp.exp(jnp.sum(lambda_q1[l] * lambda_k1[l])) - jnp.exp(jnp.sum(lambda_q2[l] * lambda_k2[l]))
               + lam_init).reshape(1).astype(F32)
        w_in_b = w_in[l].astype(BF16)
        p = {
            'norm_mix': row(norm_mix[l]),
            'w_da': w_in_b[:, :3 * DA_WIDTH],
            'w_rw': jnp.pad(w_in_b[:, 3 * DA_WIDTH:], ((0, 0), (0, N_RW_PAD - N_RW_COLS))),
            'q_gain': tile_gain(q_gain[l]), 'k_gain': tile_gain(k_gain[l]),
            'subln_gain': row(subln_gain[l]),
            'mu_shift': jnp.pad(row(mu_shift[l]), ((0, 0), (0, N_RW_PAD - N_RW_COLS))),
            'w0': row(w0[l]), 'a0': row(a0[l]),
            'w_up': jnp.pad(w_up[l], ((0, LANES - W_LORA), (0, 0))),
            'a_up': jnp.pad(a_up[l], ((W_LORA, LANES - W_LORA - A_LORA), (0, 0))),
            'g_up': jnp.pad(g_up[l], ((0, G_PAD - G_LORA), (0, 0))),
            'k_k': row(k_k[l]), 'k_a': row(k_a[l]), 'r_k': row(r_k[l]),
            'lnx_w': row(lnx_w[l]), 'lnx_b': row(lnx_b[l]),
            'w_out': w_out[l].astype(BF16), 'norm_ffn': row(norm_ffn[l]),
            'w_gate': w_gate[l].astype(BF16), 'w_up_ffn': w_up_ffn[l].astype(BF16),
            'w_down': w_down[l].astype(BF16),
        }
        S0_p = jnp.zeros((B, N_RW_HEADS, RW_HEAD, RW_HEAD), F32)
        sh0_p = jnp.zeros((B, N_RW_COLS), F32)
        xp, k_new, v_new, S_new, sh_new = _layer(xp, cos_p, sin_p, S0_p, sh0_p, lam, lam_init, p)
        outs[0].append(k_new.reshape(B, S // PAGE_SIZE, PAGE_SIZE, N_DA_SUB, DA_QK_DIM))
        outs[1].append(v_new.reshape(B, S // PAGE_SIZE, PAGE_SIZE, N_DA_HEADS, DA_V_DIM))
        outs[2].append(S_new)
        outs[3].append(sh_new)
        xs, k_new, v_new, S_new, sh_new = _layer(xs, cos_s, sin_s, state_wkv[l], state_shift[l], lam,
                                                 lam_init, p, decode=(page_table, cache_k4, cache_v4, l))
        outs[4].append(k_new.reshape(Bd, T, N_DA_SUB, DA_QK_DIM))
        outs[5].append(v_new.reshape(Bd, T, N_DA_HEADS, DA_V_DIM))
        outs[6].append(S_new)
        outs[7].append(sh_new)
    return (xp, xs) + tuple(jnp.stack(o) for o in outs)
```

```python
import functools
import math

import jax
import jax.numpy as jnp
from jax import lax
from jax.experimental import pallas as pl
from jax.experimental.pallas import tpu as pltpu

F32 = jnp.float32
BF16 = jnp.bfloat16
HIGHEST = lax.Precision.HIGHEST

D_MODEL = 2048
PAGE_SIZE = 128
DA_WIDTH = 1024
RW_WIDTH = 1024
DA_V_DIM = 128
N_DA_HEADS = 8
DA_QK_DIM = 64
N_DA_SUB = 16
RW_HEAD = 64
N_RW_HEADS = 16
N_RW_PAIRS = N_RW_HEADS // 2
W_LORA = 64
A_LORA = 64
G_LORA = 160
N_RW_COLS = 3 * RW_WIDTH + W_LORA + A_LORA + G_LORA
LANES = 128
N_RW_PAD = ((N_RW_COLS + LANES - 1) // LANES) * LANES
G_PAD = N_RW_PAD - 3 * RW_WIDTH - LANES
D_FF = 5632
ROPE_THETA = 10000.0
NORM_EPS = 1e-6
LNX_EPS = 64e-5
MASK_VALUE = -1e30
DA_SCALE = DA_QK_DIM ** -0.5
VMEM_LIMIT_BYTES = 56 * 1024 * 1024
NEW_PAD = 16


def _cp(*sem):
    return pltpu.CompilerParams(dimension_semantics=sem, vmem_limit_bytes=VMEM_LIMIT_BYTES)


def _dot(a, b, precision=None):
    return jnp.dot(a, b, preferred_element_type=F32, precision=precision)


def _dot_nt(a, b, precision=None):
    return lax.dot_general(a, b, (((1,), (1,)), ((), ())), preferred_element_type=F32,
                           precision=precision)


def _dot_tn(a, b, precision=None):
    return lax.dot_general(a, b, (((0,), (0,)), ((), ())), preferred_element_type=F32,
                           precision=precision)


def _seg_ones(n, seg):
    r = lax.broadcasted_iota(jnp.int32, (n, n), 0) // seg
    c = lax.broadcasted_iota(jnp.int32, (n, n), 1) // seg
    return (r == c).astype(F32)


def _norm_mm_kernel(x_ref, g_ref, w_ref, o_ref, xn_ref):
    @pl.when(pl.program_id(1) == 0)
    def _():
        x = x_ref[...]
        ms = jnp.mean(x * x, axis=-1, keepdims=True)
        xn_ref[...] = (x * lax.rsqrt(ms + NORM_EPS) * g_ref[...]).astype(BF16)

    o_ref[...] = _dot(xn_ref[...], w_ref[...])


def _norm_mm(x, g, w, tn):
    M, D = x.shape
    N = w.shape[1]
    tm = min(M, 1024)
    return pl.pallas_call(
        _norm_mm_kernel,
        grid=(M // tm, N // tn),
        in_specs=[pl.BlockSpec((tm, D), lambda i, j: (i, 0)),
                  pl.BlockSpec((1, D), lambda i, j: (0, 0)),
                  pl.BlockSpec((D, tn), lambda i, j: (0, j))],
        out_specs=pl.BlockSpec((tm, tn), lambda i, j: (i, j)),
        out_shape=jax.ShapeDtypeStruct((M, N), F32),
        scratch_shapes=[pltpu.VMEM((tm, D), BF16)],
        compiler_params=_cp("parallel", "arbitrary"),
        name="norm_in_proj",
    )(x, g, w)


def _qk_post_kernel(hq_ref, hk_ref, hv_ref, qg_ref, kg_ref, cos_ref, sin_ref, qb_ref, kf_ref, kb_ref, vb_ref):
    tm = hq_ref.shape[0]
    vb_ref[...] = hv_ref[...].astype(BF16)
    lane = lax.broadcasted_iota(jnp.int32, (tm, LANES), 1)
    first = (lane % DA_QK_DIM) < (DA_QK_DIM // 2)
    seg = _seg_ones(LANES, DA_QK_DIM) * (1.0 / DA_QK_DIM)
    cos = cos_ref[...]
    sin = sin_ref[...]

    def proc(x, g):
        ms = _dot(x * x, seg, HIGHEST)
        y = x * lax.rsqrt(ms + NORM_EPS) * g
        partner = jnp.where(first, pltpu.roll(y, LANES - DA_QK_DIM // 2, 1),
                            pltpu.roll(y, DA_QK_DIM // 2, 1))
        return y * cos + partner * sin

    for c in range(DA_WIDTH // LANES):
        sl = slice(LANES * c, LANES * (c + 1))
        q = proc(hq_ref[:, sl], qg_ref[...])
        qb_ref[:, sl] = (q * DA_SCALE).astype(BF16)
        k = proc(hk_ref[:, sl], kg_ref[...])
        kf_ref[:, sl] = k
        kb_ref[:, sl] = k.astype(BF16)


def _qk_post(hda, qg, kg, cos, sin, T):
    M = hda.shape[0]
    tm = min(M, 512)
    if cos.shape[0] == M:
        tab_map = lambda i: (i, 0)
    else:
        nper = T // tm
        tab_map = lambda i: (i % nper, 0)
    blk = lambda c: pl.BlockSpec((tm, DA_WIDTH), lambda i: (i, c))
    vec = pl.BlockSpec((1, LANES), lambda i: (0, 0))
    tab = pl.BlockSpec((tm, LANES), tab_map)
    return pl.pallas_call(
        _qk_post_kernel,
        grid=(M // tm,),
        in_specs=[blk(0), blk(1), blk(2), vec, vec, tab, tab],
        out_specs=[blk(0), blk(0), blk(0), blk(0)],
        out_shape=[jax.ShapeDtypeStruct((M, DA_WIDTH), BF16),
                   jax.ShapeDtypeStruct((M, DA_WIDTH), F32),
                   jax.ShapeDtypeStruct((M, DA_WIDTH), BF16),
                   jax.ShapeDtypeStruct((M, DA_WIDTH), BF16)],
        compiler_params=_cp("parallel"),
        name="qk_norm_rope",
    )(hda, hda, hda, qg, kg, cos, sin)


def _subln(o, g, lam_init):
    ms = jnp.mean(o * o, axis=-1, keepdims=True)
    return o * lax.rsqrt(ms + NORM_EPS) * g * (1.0 - lam_init)


def _attn_kernel(lam_ref, q_ref, k_ref, v_ref, g_ref, o_ref, m_scr, l_scr, acc_scr, *, tq, tk, lam_init):
    qi = pl.program_id(2)
    n_heads = q_ref.shape[2] // LANES
    lane = lax.broadcasted_iota(jnp.int32, (tq, LANES), 1)
    chains = [(hd, mi) for hd in range(n_heads) for mi in range(2)]
    qs = []
    for hd, mi in chains:
        q = q_ref[0, :, LANES * hd:LANES * (hd + 1)]
        keep = (lane < DA_QK_DIM) if mi == 0 else (lane >= DA_QK_DIM)
        qs.append(jnp.where(keep, q, jnp.zeros_like(q)))
    m_scr[...] = jnp.full(m_scr.shape, MASK_VALUE, F32)
    l_scr[...] = jnp.zeros(l_scr.shape, F32)
    acc_scr[...] = jnp.zeros(acc_scr.shape, F32)

    def block(j, masked):
        off = pl.multiple_of(j * tk, tk)
        kbs = [k_ref[0, pl.ds(off, tk), LANES * hd:LANES * (hd + 1)] for hd in range(n_heads)]
        vbs = [v_ref[0, pl.ds(off, tk), LANES * hd:LANES * (hd + 1)] for hd in range(n_heads)]
        nc = range(len(chains))
        ms = [m_scr[c] for c in nc]
        ls = [l_scr[c] for c in nc]
        accs = [acc_scr[c] for c in nc]
        ss = [_dot_nt(qs[c], kbs[chains[c][0]]) for c in nc]
        if masked:
            row = qi * tq + lax.broadcasted_iota(jnp.int32, (tq, tk), 0)
            col = off + lax.broadcasted_iota(jnp.int32, (tq, tk), 1)
            ss = [jnp.where(col <= row, s, MASK_VALUE) for s in ss]
        mns = [jnp.maximum(ms[c], jnp.max(ss[c], axis=-1, keepdims=True)) for c in nc]
        als = [jnp.exp(ms[c] - mns[c]) for c in nc]
        ps = [jnp.exp(ss[c] - mns[c]) for c in nc]
        sums = [jnp.sum(ps[c], axis=-1, keepdims=True) for c in nc]
        pvs = [_dot(ps[c].astype(BF16), vbs[chains[c][0]]) for c in nc]
        for c in nc:
            m_scr[c] = mns[c]
            l_scr[c] = ls[c] * als[c] + sums[c]
            acc_scr[c] = accs[c] * als[c] + pvs[c]

    n_full = (qi * tq) // tk

    def body(j, c):
        block(j, False)
        return c

    lax.fori_loop(0, n_full, body, 0)
    block(n_full, True)
    for hd in range(n_heads):
        c1, c2 = 2 * hd, 2 * hd + 1
        o = acc_scr[c1] / l_scr[c1] - lam_ref[0] * (acc_scr[c2] / l_scr[c2])
        o_ref[0, :, LANES * hd:LANES * (hd + 1)] = _subln(o, g_ref[...], lam_init).astype(o_ref.dtype)


def _attn_prompt(lam, qb, kb, vb, g, lam_init):
    B, S, _ = qb.shape
    tq = min(S, 128)
    tk = min(S, 256)
    hb = 2
    w = LANES * hb
    return pl.pallas_call(
        functools.partial(_attn_kernel, tq=tq, tk=tk, lam_init=lam_init),
        grid=(B, N_DA_HEADS // hb, S // tq),
        in_specs=[pl.BlockSpec(memory_space=pltpu.SMEM),
                  pl.BlockSpec((1, tq, w), lambda b, h, i: (b, i, h)),
                  pl.BlockSpec((1, S, w), lambda b, h, i: (b, 0, h)),
                  pl.BlockSpec((1, S, w), lambda b, h, i: (b, 0, h)),
                  pl.BlockSpec((1, DA_V_DIM), lambda b, h, i: (0, 0))],
        out_specs=pl.BlockSpec((1, tq, w), lambda b, h, i: (b, i, h)),
        out_shape=jax.ShapeDtypeStruct((B, S, DA_WIDTH), BF16),
        scratch_shapes=[pltpu.VMEM((2 * hb, tq, 1), F32), pltpu.VMEM((2 * hb, tq, 1), F32),
                        pltpu.VMEM((2 * hb, tq, DA_V_DIM), F32)],
        compiler_params=_cp("parallel", "parallel", "parallel"),
        name="diff_attn_prompt",
    )(lam, qb, kb, vb, g)


def _dec_attn_kernel(pt_ref, lam_ref, q_ref, kn_ref, vn_ref, g_ref, *rest, pp, n_new, lam_init):
    k_refs = rest[:pp]
    v_refs = rest[pp:2 * pp]
    o_ref = rest[2 * pp]
    m_scr, l_scr, acc_scr = rest[2 * pp + 1:]
    p_id = pl.program_id(1)
    nrow = 2 * n_new * N_DA_HEADS

    @pl.when(p_id == 0)
    def _():
        m_scr[...] = jnp.full(m_scr.shape, MASK_VALUE, F32)
        l_scr[...] = jnp.zeros(l_scr.shape, F32)
        acc_scr[...] = jnp.zeros(acc_scr.shape, F32)

    q = q_ref[0]

    def update(s, vs):
        m = m_scr[...]
        mn = jnp.maximum(m, jnp.max(s, axis=-1, keepdims=True))
        al = jnp.exp(m - mn)
        p = jnp.exp(s - mn)
        l_scr[...] = l_scr[...] * al + jnp.sum(p, axis=-1, keepdims=True)
        acc = acc_scr[...] * al
        w = s.shape[1] // len(vs)
        for i, vb in enumerate(vs):
            acc = acc + _dot(p[:, i * w:(i + 1) * w].astype(BF16), vb)
        acc_scr[...] = acc
        m_scr[...] = mn

    s_all = jnp.concatenate([_dot_nt(q, k_refs[i][0, 0].astype(BF16)) for i in range(pp)], axis=1)
    update(s_all, [v_refs[i][0, 0].astype(BF16) for i in range(pp)])

    @pl.when(p_id == pl.num_programs(1) - 1)
    def _():
        s = _dot_nt(q, kn_ref[0].astype(BF16))
        r = lax.broadcasted_iota(jnp.int32, s.shape, 0)
        t = lax.broadcasted_iota(jnp.int32, s.shape, 1)
        qq = (r % (n_new * N_DA_HEADS)) // N_DA_HEADS
        s = jnp.where(t <= qq, s, MASK_VALUE)
        update(s, [vn_ref[0].astype(BF16)])
        n = acc_scr[...] / l_scr[...]
        half = nrow // 2
        comb = n[:half] - lam_ref[0] * n[half:]
        rr = lax.broadcasted_iota(jnp.int32, comb.shape, 0)
        cc = lax.broadcasted_iota(jnp.int32, comb.shape, 1)
        comb = jnp.where((rr % N_DA_HEADS) == (cc // DA_V_DIM), comb, 0.0)
        sr = lax.broadcasted_iota(jnp.int32, (8, half), 0)
        sc = lax.broadcasted_iota(jnp.int32, (8, half), 1)
        sel = (sc // N_DA_HEADS == sr).astype(F32)
        out = _dot(sel, comb, HIGHEST)
        for c in range(N_DA_HEADS):
            sl = slice(DA_V_DIM * c, DA_V_DIM * (c + 1))
            o_ref[0, :, sl] = _subln(out[:n_new, sl], g_ref[...], lam_init)


def _attn_sample(page_table, lam, qrows, k_new, v_new, g, cache_k4, cache_v4, layer, lam_init, pp=4):
    Bd, nrow, _ = qrows.shape
    n_new = nrow // (2 * N_DA_HEADS)
    n_pages = page_table.shape[1]
    pt = page_table.reshape(-1)

    def page_spec(i):
        return pl.BlockSpec((1, 1, PAGE_SIZE, DA_WIDTH),
                            lambda b, p, pt_ref: (layer, pt_ref[b * n_pages + p * pp + i], 0, 0))

    row_spec = lambda r: pl.BlockSpec((1, r, DA_WIDTH), lambda b, p, pt_ref: (b, 0, 0))
    grid_spec = pltpu.PrefetchScalarGridSpec(
        num_scalar_prefetch=1,
        grid=(Bd, n_pages // pp),
        in_specs=[pl.BlockSpec(memory_space=pltpu.SMEM),
                  row_spec(nrow), row_spec(NEW_PAD), row_spec(NEW_PAD),
                  pl.BlockSpec((1, DA_V_DIM), lambda b, p, pt_ref: (0, 0))]
                 + [page_spec(i) for i in range(pp)] + [page_spec(i) for i in range(pp)],
        out_specs=row_spec(n_new),
        scratch_shapes=[pltpu.VMEM((nrow, 1), F32), pltpu.VMEM((nrow, 1), F32),
                        pltpu.VMEM((nrow, DA_WIDTH), F32)],
    )
    return pl.pallas_call(
        functools.partial(_dec_attn_kernel, pp=pp, n_new=n_new, lam_init=lam_init),
        grid_spec=grid_spec,
        out_shape=jax.ShapeDtypeStruct((Bd, n_new, DA_WIDTH), F32),
        compiler_params=_cp("parallel", "arbitrary"),
        name="diff_attn_decode",
    )(pt, lam, qrows, k_new, v_new, g, *([cache_k4] * pp), *([cache_v4] * pp))


def _rw_prep_kernel(h_ref, sh0_ref, mu_ref, w0_ref, a0_ref, wup_ref, aup_ref, gup_ref,
                    r_ref, k_ref, v_ref, lw_ref, a_ref, g_ref, prev_scr):
    cp = h_ref.shape[1]

    @pl.when(pl.program_id(1) == 0)
    def _():
        prev_scr[...] = sh0_ref[0]

    rw = h_ref[0]
    rowi = lax.broadcasted_iota(jnp.int32, rw.shape, 0)
    prev = jnp.where(rowi == 0, prev_scr[...], pltpu.roll(rw, 1, 0))
    prev_scr[...] = rw[cp - 1:cp, :]
    xm = rw + (prev - rw) * mu_ref[...]
    c1, c2, c3 = RW_WIDTH, 2 * RW_WIDTH, 3 * RW_WIDTH
    r_ref[0] = xm[:, :c1]
    k_ref[0] = xm[:, c1:c2]
    v_ref[0] = xm[:, c2:c3]
    lo = xm[:, c3:c3 + LANES]
    dg = xm[:, c3 + LANES:]
    z = -(w0_ref[...] + _dot(jnp.tanh(lo), wup_ref[...], HIGHEST))
    softplus = jnp.maximum(z, 0.0) + jnp.log1p(jnp.exp(-jnp.abs(z)))
    lw_ref[0] = -jnp.exp(-softplus - 0.5)
    a_ref[0] = jax.nn.sigmoid(a0_ref[...] + _dot(lo, aup_ref[...], HIGHEST))
    g_ref[0] = _dot(jax.nn.sigmoid(dg), gup_ref[...], HIGHEST)


def _rw_prep(hrw3, sh0, mu, w0, a0, wup, aup, gup):
    B, T, _ = hrw3.shape
    cp = min(T, 256)
    full = lambda shape: pl.BlockSpec(shape, lambda b, c: (0,) * len(shape))
    out_blk = pl.BlockSpec((1, cp, RW_WIDTH), lambda b, c: (b, c, 0))
    out_sds = jax.ShapeDtypeStruct((B, T, RW_WIDTH), F32)
    return pl.pallas_call(
        _rw_prep_kernel,
        grid=(B, T // cp),
        in_specs=[pl.BlockSpec((1, cp, N_RW_PAD), lambda b, c: (b, c, 0)),
                  pl.BlockSpec((1, 1, N_RW_PAD), lambda b, c: (b, 0, 0)),
                  full((1, N_RW_PAD)), full((1, RW_WIDTH)), full((1, RW_WIDTH)),
                  full((LANES, RW_WIDTH)), full((LANES, RW_WIDTH)), full((G_PAD, RW_WIDTH))],
        out_specs=[out_blk] * 6,
        out_shape=[out_sds] * 6,
        scratch_shapes=[pltpu.VMEM((1, N_RW_PAD), F32)],
        compiler_params=_cp("parallel", "arbitrary"),
        name="rwkv_prep",
    )(hrw3, sh0, mu, w0, a0, wup, aup, gup)


def _split(x):
    hi = x.astype(BF16)
    lo = (x - hi.astype(F32)).astype(BF16)
    return hi, lo


def _mm3(a, b, dot=_dot):
    return dot(a[1], b[0]) + dot(a[0], b[1]) + dot(a[0], b[0])


def _mm_exact_rhs(x, m):
    hi, lo = _split(x)
    return _dot(lo, m) + _dot(hi, m)


def _rw_chunk(args, masks, t_valid):
    C = args[0][0].shape[0]
    incl_b, strict, incl, eye, seg_b, head_a, blockdiag, diag, levels = masks
    head_masks = (head_a, jnp.logical_not(head_a))
    P = range(len(args))
    PH = [(p, hd) for p in P for hd in range(2)]

    pre = []
    for (r, k, v, lw, a, g, kkp, kap, rkp, lnw, lnb, H) in args:
        if t_valid < C:
            valid = lax.broadcasted_iota(jnp.int32, (C, LANES), 0) < t_valid
            lw = jnp.where(valid, lw, 0.0)
            k = jnp.where(valid, k, 0.0)
            v = jnp.where(valid, v, 0.0)
        kk = k * kkp
        kk = kk * lax.rsqrt(jnp.maximum(_mm_exact_rhs(kk * kk, seg_b), 1e-24))
        k2 = k * (1.0 + (a - 1.0) * kap)
        b = kk * a
        lw_hi, lw_lo = _split(lw)
        lw_lo2 = (lw - lw_hi.astype(F32) - lw_lo.astype(F32)).astype(BF16)
        gam = _dot(incl_b, lw_lo2) + _dot(incl_b, lw_lo) + _dot(incl_b, lw_hi)
        gl = gam[C - 1:C, :]
        rt = r * jnp.exp(gam)
        at = -kk * jnp.exp(gam - lw)
        ginv = jnp.exp(-gam)
        gh = jnp.exp(gl - gam)
        pre.append(dict(r=r, v=v, k2=k2, g=g, rkp=rkp, lnw=lnw, lnb=lnb, H=H, gl=gl, rt=rt, at=at,
                        bt=_split(b * ginv), kt=_split(k2 * ginv), bh=b * gh, kh=k2 * gh,
                        Hs=_split(H), vs=_split(v)))

    atm = {(p, hd): _split(jnp.where(head_masks[hd], pre[p]['at'], 0.0)) for p, hd in PH}
    rtm = {(p, hd): _split(jnp.where(head_masks[hd], pre[p]['rt'], 0.0)) for p, hd in PH}
    aab = {c: jnp.where(strict, _mm3(atm[c], pre[c[0]]['bt'], _dot_nt), 0.0) for c in PH}
    tinv = {c: eye + jnp.where(levels[0], aab[c], 0.0) for c in PH}
    aak = {c: jnp.where(strict, _mm3(atm[c], pre[c[0]]['kt'], _dot_nt), 0.0) for c in PH}
    X = [_mm3(_split(pre[p]['at']), pre[p]['Hs']) for p in P]
    for lvl in levels[1:]:
        ts = {c: _split(tinv[c]) for c in PH}
        mid = {c: _split(_mm3(_split(jnp.where(lvl, aab[c], 0.0)), ts[c])) for c in PH}
        tinv = {c: tinv[c] + _mm3(ts[c], mid[c]) for c in PH}
    ys = {c: _mm3(_split(aak[c]), pre[c[0]]['vs']) for c in PH}
    arb = {c: _split(jnp.where(incl, _mm3(rtm[c], pre[c[0]]['bt'], _dot_nt), 0.0)) for c in PH}
    ark = {c: _split(jnp.where(incl, _mm3(rtm[c], pre[c[0]]['kt'], _dot_nt), 0.0)) for c in PH}
    RH = [_mm3(_split(pre[p]['rt']), pre[p]['Hs']) for p in P]
    ysum = [_split(X[p] + jnp.where(head_a, ys[(p, 0)], ys[(p, 1)])) for p in P]
    u = {c: _mm3(_split(tinv[c]), ysum[c[0]]) for c in PH}
    U = [jnp.where(head_a, u[(p, 0)], u[(p, 1)]) for p in P]
    Us = [_split(U[p]) for p in P]

    hn = []
    for p in P:
        dg = jnp.where(diag, jnp.exp(pre[p]['gl']), 0.0)
        lhs = _split(jnp.concatenate([pre[p]['bh'], pre[p]['kh'], dg], axis=0))
        rhs = _split(jnp.concatenate([U[p], pre[p]['v'], pre[p]['H']], axis=0))
        hn.append(jnp.where(blockdiag, _mm3(lhs, rhs, _dot_tn), 0.0))

    dd = {c: _mm3(arb[c], Us[c[0]]) + _mm3(ark[c], pre[c[0]]['vs']) for c in PH}
    inv_n = 1.0 / RW_HEAD
    outs = []
    for p in P:
        q = pre[p]
        O = RH[p] + jnp.where(head_a, dd[(p, 0)], dd[(p, 1)])
        mean = _mm_exact_rhs(O, seg_b) * inv_n
        d = O - mean
        var = _mm_exact_rhs(d * d, seg_b) * inv_n
        on = d * lax.rsqrt(var + LNX_EPS) * q['lnw'] + q['lnb']
        bonus = _mm_exact_rhs(q['r'] * q['k2'] * q['rkp'], seg_b) * q['v']
        outs.append(((on + bonus) * q['g'], hn[p]))
    return outs


def _rw_scan_kernel(r_ref, k_ref, v_ref, lw_ref, a_ref, g_ref, kk_ref, ka_ref, rk_ref,
                    lnw_ref, lnb_ref, h0_ref, o_ref, hout_ref, h_scr, *, t_valid):
    C = r_ref.shape[1]
    n_pairs = r_ref.shape[2] // LANES
    ci = pl.program_id(2)

    @pl.when(ci == 0)
    def _():
        h_scr[...] = h0_ref[0]

    ti = lax.broadcasted_iota(jnp.int32, (C, C), 0)
    si = lax.broadcasted_iota(jnp.int32, (C, C), 1)
    hr = lax.broadcasted_iota(jnp.int32, (LANES, LANES), 0)
    hc = lax.broadcasted_iota(jnp.int32, (LANES, LANES), 1)
    blockdiag = (hr // RW_HEAD) == (hc // RW_HEAD)
    levels = []
    m = 1
    while m < C:
        levels.append(((ti // (2 * m)) == (si // (2 * m))) & ((ti % (2 * m)) >= m) & ((si % (2 * m)) < m))
        m *= 2
    masks = ((si <= ti).astype(BF16), si < ti, si <= ti, (si == ti).astype(F32),
             blockdiag.astype(BF16),
             lax.broadcasted_iota(jnp.int32, (C, LANES), 1) < RW_HEAD, blockdiag, hr == hc, levels)
    sls = [slice(LANES * pi, LANES * (pi + 1)) for pi in range(n_pairs)]
    args = [(r_ref[0, :, sl], k_ref[0, :, sl], v_ref[0, :, sl], lw_ref[0, :, sl], a_ref[0, :, sl],
             g_ref[0, :, sl], kk_ref[:, sl], ka_ref[:, sl], rk_ref[:, sl], lnw_ref[:, sl], lnb_ref[:, sl],
             h_scr[pi]) for pi, sl in enumerate(sls)]
    res = _rw_chunk(args, masks, t_valid)
    for pi, sl in enumerate(sls):
        o_ref[0, :, sl] = res[pi][0].astype(o_ref.dtype)
        h_scr[pi] = res[pi][1]

    @pl.when(ci == pl.num_programs(2) - 1)
    def _():
        hout_ref[0] = h_scr[...]


def _rw_scan(r, k, v, lw, a, g, kk, ka, rk, lnw, lnb, h0, t_valid, out_dtype):
    B, T, _ = r.shape
    C = min(T, 128)
    pb = 4
    w = LANES * pb
    tok = pl.BlockSpec((1, C, w), lambda b, p, c: (b, c, p))
    vec = pl.BlockSpec((1, w), lambda b, p, c: (0, p))
    st = pl.BlockSpec((1, pb, LANES, LANES), lambda b, p, c: (b, p, 0, 0))
    return pl.pallas_call(
        functools.partial(_rw_scan_kernel, t_valid=t_valid),
        grid=(B, N_RW_PAIRS // pb, T // C),
        in_specs=[tok] * 6 + [vec] * 5 + [st],
        out_specs=[tok, st],
        out_shape=[jax.ShapeDtypeStruct((B, T, RW_WIDTH), out_dtype),
                   jax.ShapeDtypeStruct((B, N_RW_PAIRS, LANES, LANES), F32)],
        scratch_shapes=[pltpu.VMEM((pb, LANES, LANES), F32)],
        compiler_params=_cp("parallel", "parallel", "arbitrary"),
        name="rwkv_scan",
    )(r, k, v, lw, a, g, kk, ka, rk, lnw, lnb, h0)


def _out_proj_kernel(a1_ref, a2_ref, w_ref, res_ref, o_ref):
    acc = _dot(a1_ref[...].astype(BF16), w_ref[:DA_WIDTH, :])
    acc = acc + _dot(a2_ref[...].astype(BF16), w_ref[DA_WIDTH:, :])
    o_ref[...] = res_ref[...] + acc


def _out_proj(a1, a2, w, res):
    M, D = res.shape
    tm = min(M, 1024)
    tn = 512
    return pl.pallas_call(
        _out_proj_kernel,
        grid=(M // tm, D // tn),
        in_specs=[pl.BlockSpec((tm, DA_WIDTH), lambda i, j: (i, 0)),
                  pl.BlockSpec((tm, RW_WIDTH), lambda i, j: (i, 0)),
                  pl.BlockSpec((DA_WIDTH + RW_WIDTH, tn), lambda i, j: (0, j)),
                  pl.BlockSpec((tm, tn), lambda i, j: (i, j))],
        out_specs=pl.BlockSpec((tm, tn), lambda i, j: (i, j)),
        out_shape=jax.ShapeDtypeStruct((M, D), F32),
        compiler_params=_cp("parallel", "parallel"),
        name="out_proj",
    )(a1, a2, w, res)


def _ffn_up_kernel(x_ref, g_ref, wg_ref, wu_ref, o_ref, xn_ref):
    @pl.when(pl.program_id(1) == 0)
    def _():
        x = x_ref[...]
        ms = jnp.mean(x * x, axis=-1, keepdims=True)
        xn_ref[...] = (x * lax.rsqrt(ms + NORM_EPS) * g_ref[...]).astype(BF16)

    xn = xn_ref[...]
    gate = _dot(xn, wg_ref[...])
    up = _dot(xn, wu_ref[...])
    o_ref[...] = (gate * jax.nn.sigmoid(gate) * up).astype(o_ref.dtype)


def _ffn_up(x, g, wg, wu):
    M, D = x.shape
    tm = min(M, 1024)
    tn = 512
    return pl.pallas_call(
        _ffn_up_kernel,
        grid=(M // tm, D_FF // tn),
        in_specs=[pl.BlockSpec((tm, D), lambda i, j: (i, 0)),
                  pl.BlockSpec((1, D), lambda i, j: (0, 0)),
                  pl.BlockSpec((D, tn), lambda i, j: (0, j)),
                  pl.BlockSpec((D, tn), lambda i, j: (0, j))],
        out_specs=pl.BlockSpec((tm, tn), lambda i, j: (i, j)),
        out_shape=jax.ShapeDtypeStruct((M, D_FF), BF16),
        scratch_shapes=[pltpu.VMEM((tm, D), BF16)],
        compiler_params=_cp("parallel", "arbitrary"),
        name="ffn_gate_up",
    )(x, g, wg, wu)


def _ffn_down_kernel(a_ref, w_ref, res_ref, o_ref):
    o_ref[...] = res_ref[...] + _dot(a_ref[...], w_ref[...])


def _ffn_down(a, w, res):
    M, D = res.shape
    tm = min(M, 1024)
    tn = 512
    return pl.pallas_call(
        _ffn_down_kernel,
        grid=(M // tm, D // tn),
        in_specs=[pl.BlockSpec((tm, D_FF), lambda i, j: (i, 0)),
                  pl.BlockSpec((D_FF, tn), lambda i, j: (0, j)),
                  pl.BlockSpec((tm, tn), lambda i, j: (i, j))],
        out_specs=pl.BlockSpec((tm, tn), lambda i, j: (i, j)),
        out_shape=jax.ShapeDtypeStruct((M, D), F32),
        compiler_params=_cp("parallel", "parallel"),
        name="ffn_down",
    )(a, w, res)


def _rope_tables(pos):
    half = DA_QK_DIM // 2
    inv_freq = jnp.power(ROPE_THETA, -jnp.arange(half, dtype=F32) / half)
    ang = pos.astype(F32)[:, None] * inv_freq[None, :]
    cos = jnp.cos(ang)
    sin = jnp.sin(ang)
    reps = LANES // DA_QK_DIM
    cos_t = jnp.tile(jnp.concatenate([cos, cos], axis=-1), (1, reps))
    sin_t = jnp.tile(jnp.concatenate([-sin, sin], axis=-1), (1, reps))
    return cos_t, sin_t


def _state_to_pairs(S):
    B = S.shape[0]
    St = jnp.swapaxes(S, -1, -2).reshape(B, N_RW_PAIRS, 2, RW_HEAD, RW_HEAD)
    eye2 = jnp.eye(2, dtype=S.dtype)
    Hp = St[:, :, :, :, None, :] * eye2[None, None, :, None, :, None]
    return Hp.reshape(B, N_RW_PAIRS, LANES, LANES)


def _pairs_to_state(Hp):
    B = Hp.shape[0]
    H5 = Hp.reshape(B, N_RW_PAIRS, 2, RW_HEAD, 2, RW_HEAD)
    diag = jnp.stack([H5[:, :, 0, :, 0, :], H5[:, :, 1, :, 1, :]], axis=2)
    return jnp.swapaxes(diag.reshape(B, N_RW_HEADS, RW_HEAD, RW_HEAD), -1, -2)


def _decode_query_rows(qb):
    Bd, n_new, _ = qb.shape
    sub = jnp.arange(DA_WIDTH) // DA_QK_DIM
    want = (2 * jnp.arange(N_DA_HEADS)[None, :] + jnp.arange(2)[:, None])
    mask = (sub[None, None, :] == want[:, :, None])
    rows = jnp.where(mask[None, :, None, :, :], qb[:, None, :, None, :], jnp.zeros((), qb.dtype))
    return rows.reshape(Bd, 2 * n_new * N_DA_HEADS, DA_WIDTH)


def _pad_rows(x, n):
    return jnp.pad(x, ((0, 0), (0, n - x.shape[1]), (0, 0)))


def _layer(x3, cos, sin, S0, shift0, lam, lam_init, p, decode=None):
    B, T, D = x3.shape
    M = B * T
    x = x3.reshape(M, D)
    hda = _norm_mm(x, p['norm_mix'], p['w_da'], 1024)
    hrw = _norm_mm(x, p['norm_mix'], p['w_rw'], 1152)
    qb, kf, kb, vb = _qk_post(hda, p['q_gain'], p['k_gain'], cos, sin, T)
    hda3 = hda.reshape(B, T, 3 * DA_WIDTH)
    v_new = hda3[:, :, 2 * DA_WIDTH:]
    if decode is None:
        o_da = _attn_prompt(lam, qb.reshape(B, T, DA_WIDTH), kb.reshape(B, T, DA_WIDTH),
                            vb.reshape(B, T, DA_WIDTH), p['subln_gain'], lam_init)
    else:
        page_table, cache_k4, cache_v4, layer = decode
        qrows = _decode_query_rows(qb.reshape(B, T, DA_WIDTH))
        o_da = _attn_sample(page_table, lam, qrows, _pad_rows(kf.reshape(B, T, DA_WIDTH), NEW_PAD),
                            _pad_rows(v_new, NEW_PAD), p['subln_gain'], cache_k4, cache_v4,
                            layer, lam_init)
    hrw3 = hrw.reshape(B, T, N_RW_PAD)
    shift_new = hrw3[:, -1, :N_RW_COLS]
    t_pad = max(T, 8)
    hrw3p = _pad_rows(hrw3, t_pad)
    sh0 = jnp.pad(shift0, ((0, 0), (0, N_RW_PAD - N_RW_COLS)))[:, None, :]
    r, k, v, lw, a, g = _rw_prep(hrw3p, sh0, p['mu_shift'], p['w0'], p['a0'], p['w_up'], p['a_up'],
                                 p['g_up'])
    o_rw, hp = _rw_scan(r, k, v, lw, a, g, p['k_k'], p['k_a'], p['r_k'], p['lnx_w'], p['lnx_b'],
                        _state_to_pairs(S0), T, BF16 if decode is None else F32)
    o_rw = o_rw[:, :T]
    S_new = _pairs_to_state(hp)
    x1 = _out_proj(o_da.reshape(M, DA_WIDTH), o_rw.reshape(M, RW_WIDTH), p['w_out'], x)
    ff = _ffn_up(x1, p['norm_ffn'], p['w_gate'], p['w_up_ffn'])
    x2 = _ffn_down(ff, p['w_down'], x1)
    return x2.reshape(B, T, D), kf, v_new, S_new, shift_new


def kernel(x_prompt, x_sample, cache_k, cache_v, state_wkv, state_shift, page_table, norm_mix, w_in, q_gain, k_gain, lambda_q1, lambda_k1, lambda_q2, lambda_k2, subln_gain, mu_shift, w0, w_up, a0, a_up, g_up, k_k, k_a, r_k, lnx_w, lnx_b, w_out, norm_ffn, w_gate, w_up_ffn, w_down):
    B, S, _ = x_prompt.shape
    Bd, T, _ = x_sample.shape
    depth = w_in.shape[0]
    n_pages = page_table.shape[1]
    past_len = n_pages * PAGE_SIZE
    cos_p, sin_p = _rope_tables(jnp.arange(S, dtype=jnp.int32))
    cos_s, sin_s = _rope_tables(past_len + jnp.arange(T, dtype=jnp.int32))
    cos_s = jnp.tile(cos_s, (Bd, 1))
    sin_s = jnp.tile(sin_s, (Bd, 1))
    n_pool = cache_k.shape[1]
    cache_k4 = cache_k.reshape(depth, n_pool, PAGE_SIZE, DA_WIDTH)
    cache_v4 = cache_v.reshape(depth, n_pool, PAGE_SIZE, DA_WIDTH)
    row = lambda t: t.reshape(1, -1)
    tile_gain = lambda t: jnp.tile(t, LANES // DA_QK_DIM).reshape(1, LANES)

    xp, xs = x_prompt, x_sample
    outs = [[] for _ in range(8)]
    for l in range(depth):
        lam_init = 0.8 - 0.6 * math.exp(-0.3 * l)
        lam = (jnp.exp(jnp.sum(lambda_q1[l] * lambda_k1[l])) - jnp.exp(jnp.sum(lambda_q2[l] * lambda_k2[l]))
               + lam_init).reshape(1).astype(F32)
        w_in_b = w_in[l].astype(BF16)
        p = {
            'norm_mix': row(norm_mix[l]),
            'w_da': w_in_b[:, :3 * DA_WIDTH],
            'w_rw': jnp.pad(w_in_b[:, 3 * DA_WIDTH:], ((0, 0), (0, N_RW_PAD - N_RW_COLS))),
            'q_gain': tile_gain(q_gain[l]), 'k_gain': tile_gain(k_gain[l]),
            'subln_gain': row(subln_gain[l]),
            'mu_shift': jnp.pad(row(mu_shift[l]), ((0, 0), (0, N_RW_PAD - N_RW_COLS))),
            'w0': row(w0[l]), 'a0': row(a0[l]),
            'w_up': jnp.pad(w_up[l], ((0, LANES - W_LORA), (0, 0))),
            'a_up': jnp.pad(a_up[l], ((W_LORA, LANES - W_LORA - A_LORA), (0, 0))),
            'g_up': jnp.pad(g_up[l], ((0, G_PAD - G_LORA), (0, 0))),
            'k_k': row(k_k[l]), 'k_a': row(k_a[l]), 'r_k': row(r_k[l]),
            'lnx_w': row(lnx_w[l]), 'lnx_b': row(lnx_b[l]),
            'w_out': w_out[l].astype(BF16), 'norm_ffn': row(norm_ffn[l]),
            'w_gate': w_gate[l].astype(BF16), 'w_up_ffn': w_up_ffn[l].astype(BF16),
            'w_down': w_down[l].astype(BF16),
        }
        S0_p = jnp.zeros((B, N_RW_HEADS, RW_HEAD, RW_HEAD), F32)
        sh0_p = jnp.zeros((B, N_RW_COLS), F32)
        xp, k_new, v_new, S_new, sh_new = _layer(xp, cos_p, sin_p, S0_p, sh0_p, lam, lam_init, p)
        outs[0].append(k_new.reshape(B, S // PAGE_SIZE, PAGE_SIZE, N_DA_SUB, DA_QK_DIM))
        outs[1].append(v_new.reshape(B, S // PAGE_SIZE, PAGE_SIZE, N_DA_HEADS, DA_V_DIM))
        outs[2].append(S_new)
        outs[3].append(sh_new)
        xs, k_new, v_new, S_new, sh_new = _layer(xs, cos_s, sin_s, state_wkv[l], state_shift[l], lam,
                                                 lam_init, p, decode=(page_table, cache_k4, cache_v4, l))
        outs[4].append(k_new.reshape(Bd, T, N_DA_SUB, DA_QK_DIM))
        outs[5].append(v_new.reshape(Bd, T, N_DA_HEADS, DA_V_DIM))
        outs[6].append(S_new)
        outs[7].append(sh_new)
    return (xp, xs) + tuple(jnp.stack(o) for o in outs)
```

```python
import functools
import math

import jax
import jax.numpy as jnp
from jax import lax
from jax.experimental import pallas as pl
from jax.experimental.pallas import tpu as pltpu

F32 = jnp.float32
BF16 = jnp.bfloat16
HIGHEST = lax.Precision.HIGHEST

D_MODEL = 2048
PAGE_SIZE = 128
DA_WIDTH = 1024
RW_WIDTH = 1024
DA_V_DIM = 128
N_DA_HEADS = 8
DA_QK_DIM = 64
N_DA_SUB = 16
RW_HEAD = 64
N_RW_HEADS = 16
N_RW_PAIRS = N_RW_HEADS // 2
W_LORA = 64
A_LORA = 64
G_LORA = 160
N_RW_COLS = 3 * RW_WIDTH + W_LORA + A_LORA + G_LORA
LANES = 128
N_RW_PAD = ((N_RW_COLS + LANES - 1) // LANES) * LANES
G_PAD = N_RW_PAD - 3 * RW_WIDTH - LANES
D_FF = 5632
ROPE_THETA = 10000.0
NORM_EPS = 1e-6
LNX_EPS = 64e-5
MASK_VALUE = -1e30
DA_SCALE = DA_QK_DIM ** -0.5
VMEM_LIMIT_BYTES = 56 * 1024 * 1024


def _cp(*sem):
    return pltpu.CompilerParams(dimension_semantics=sem, vmem_limit_bytes=VMEM_LIMIT_BYTES)


def _dot(a, b, precision=None):
    return jnp.dot(a, b, preferred_element_type=F32, precision=precision)


def _dot_nt(a, b, precision=None):
    return lax.dot_general(a, b, (((1,), (1,)), ((), ())), preferred_element_type=F32,
                           precision=precision)


def _dot_tn(a, b, precision=None):
    return lax.dot_general(a, b, (((0,), (0,)), ((), ())), preferred_element_type=F32,
                           precision=precision)


def _seg_ones(n, seg):
    r = lax.broadcasted_iota(jnp.int32, (n, n), 0) // seg
    c = lax.broadcasted_iota(jnp.int32, (n, n), 1) // seg
    return (r == c).astype(F32)


def _norm_mm_kernel(x_ref, g_ref, w_ref, o_ref, xn_ref):
    @pl.when(pl.program_id(1) == 0)
    def _():
        x = x_ref[...]
        ms = jnp.mean(x * x, axis=-1, keepdims=True)
        xn_ref[...] = (x * lax.rsqrt(ms + NORM_EPS) * g_ref[...]).astype(BF16)

    o_ref[...] = _dot(xn_ref[...], w_ref[...])


def _norm_mm(x, g, w, tn):
    M, D = x.shape
    N = w.shape[1]
    tm = min(M, 1024)
    return pl.pallas_call(
        _norm_mm_kernel,
        grid=(M // tm, N // tn),
        in_specs=[pl.BlockSpec((tm, D), lambda i, j: (i, 0)),
                  pl.BlockSpec((1, D), lambda i, j: (0, 0)),
                  pl.BlockSpec((D, tn), lambda i, j: (0, j))],
        out_specs=pl.BlockSpec((tm, tn), lambda i, j: (i, j)),
        out_shape=jax.ShapeDtypeStruct((M, N), F32),
        scratch_shapes=[pltpu.VMEM((tm, D), BF16)],
        compiler_params=_cp("parallel", "arbitrary"),
        name="norm_in_proj",
    )(x, g, w)


def _qk_post_kernel(hq_ref, hk_ref, hv_ref, qg_ref, kg_ref, cos_ref, sin_ref, qb_ref, kf_ref, kb_ref, vb_ref):
    tm = hq_ref.shape[0]
    vb_ref[...] = hv_ref[...].astype(BF16)
    lane = lax.broadcasted_iota(jnp.int32, (tm, LANES), 1)
    first = (lane % DA_QK_DIM) < (DA_QK_DIM // 2)
    seg = _seg_ones(LANES, DA_QK_DIM) * (1.0 / DA_QK_DIM)
    cos = cos_ref[...]
    sin = sin_ref[...]

    def proc(x, g):
        ms = _dot(x * x, seg, HIGHEST)
        y = x * lax.rsqrt(ms + NORM_EPS) * g
        partner = jnp.where(first, pltpu.roll(y, LANES - DA_QK_DIM // 2, 1),
                            pltpu.roll(y, DA_QK_DIM // 2, 1))
        return y * cos + partner * sin

    for c in range(DA_WIDTH // LANES):
        sl = slice(LANES * c, LANES * (c + 1))
        q = proc(hq_ref[:, sl], qg_ref[...])
        qb_ref[:, sl] = (q * DA_SCALE).astype(BF16)
        k = proc(hk_ref[:, sl], kg_ref[...])
        kf_ref[:, sl] = k
        kb_ref[:, sl] = k.astype(BF16)


def _qk_post(hda, qg, kg, cos, sin, T):
    M = hda.shape[0]
    tm = min(M, 512)
    if cos.shape[0] == M:
        tab_map = lambda i: (i, 0)
    else:
        nper = T // tm
        tab_map = lambda i: (i % nper, 0)
    blk = lambda c: pl.BlockSpec((tm, DA_WIDTH), lambda i: (i, c))
    vec = pl.BlockSpec((1, LANES), lambda i: (0, 0))
    tab = pl.BlockSpec((tm, LANES), tab_map)
    return pl.pallas_call(
        _qk_post_kernel,
        grid=(M // tm,),
        in_specs=[blk(0), blk(1), blk(2), vec, vec, tab, tab],
        out_specs=[blk(0), blk(0), blk(0), blk(0)],
        out_shape=[jax.ShapeDtypeStruct((M, DA_WIDTH), BF16),
                   jax.ShapeDtypeStruct((M, DA_WIDTH), F32),
                   jax.ShapeDtypeStruct((M, DA_WIDTH), BF16),
                   jax.ShapeDtypeStruct((M, DA_WIDTH), BF16)],
        compiler_params=_cp("parallel"),
        name="qk_norm_rope",
    )(hda, hda, hda, qg, kg, cos, sin)


def _subln(o, g, lam_init):
    ms = jnp.mean(o * o, axis=-1, keepdims=True)
    return o * lax.rsqrt(ms + NORM_EPS) * g * (1.0 - lam_init)


def _attn_kernel(lam_ref, q_ref, k_ref, v_ref, g_ref, o_ref, m_scr, l_scr, acc_scr, *, tq, tk, lam_init):
    qi = pl.program_id(2)
    n_heads = q_ref.shape[2] // LANES
    lane = lax.broadcasted_iota(jnp.int32, (tq, LANES), 1)
    chains = [(hd, mi) for hd in range(n_heads) for mi in range(2)]
    qs = []
    for hd, mi in chains:
        q = q_ref[0, :, LANES * hd:LANES * (hd + 1)]
        keep = (lane < DA_QK_DIM) if mi == 0 else (lane >= DA_QK_DIM)
        qs.append(jnp.where(keep, q, jnp.zeros_like(q)))
    m_scr[...] = jnp.full(m_scr.shape, MASK_VALUE, F32)
    l_scr[...] = jnp.zeros(l_scr.shape, F32)
    acc_scr[...] = jnp.zeros(acc_scr.shape, F32)

    def block(j, masked):
        off = pl.multiple_of(j * tk, tk)
        kbs = [k_ref[0, pl.ds(off, tk), LANES * hd:LANES * (hd + 1)] for hd in range(n_heads)]
        vbs = [v_ref[0, pl.ds(off, tk), LANES * hd:LANES * (hd + 1)] for hd in range(n_heads)]
        nc = range(len(chains))
        ms = [m_scr[c] for c in nc]
        ls = [l_scr[c] for c in nc]
        accs = [acc_scr[c] for c in nc]
        ss = [_dot_nt(qs[c], kbs[chains[c][0]]) for c in nc]
        if masked:
            row = qi * tq + lax.broadcasted_iota(jnp.int32, (tq, tk), 0)
            col = off + lax.broadcasted_iota(jnp.int32, (tq, tk), 1)
            ss = [jnp.where(col <= row, s, MASK_VALUE) for s in ss]
        mns = [jnp.maximum(ms[c], jnp.max(ss[c], axis=-1, keepdims=True)) for c in nc]
        als = [jnp.exp(ms[c] - mns[c]) for c in nc]
        ps = [jnp.exp(ss[c] - mns[c]) for c in nc]
        sums = [jnp.sum(ps[c], axis=-1, keepdims=True) for c in nc]
        pvs = [_dot(ps[c].astype(BF16), vbs[chains[c][0]]) for c in nc]
        for c in nc:
            m_scr[c] = mns[c]
            l_scr[c] = ls[c] * als[c] + sums[c]
            acc_scr[c] = accs[c] * als[c] + pvs[c]

    n_full = (qi * tq) // tk

    def body(j, c):
        block(j, False)
        return c

    lax.fori_loop(0, n_full, body, 0)
    block(n_full, True)
    for hd in range(n_heads):
        c1, c2 = 2 * hd, 2 * hd + 1
        o = acc_scr[c1] / l_scr[c1] - lam_ref[0] * (acc_scr[c2] / l_scr[c2])
        o_ref[0, :, LANES * hd:LANES * (hd + 1)] = _subln(o, g_ref[...], lam_init).astype(o_ref.dtype)


def _attn_prompt(lam, qb, kb, vb, g, lam_init):
    B, S, _ = qb.shape
    tq = min(S, 128)
    tk = min(S, 256)
    hb = 2
    w = LANES * hb
    return pl.pallas_call(
        functools.partial(_attn_kernel, tq=tq, tk=tk, lam_init=lam_init),
        grid=(B, N_DA_HEADS // hb, S // tq),
        in_specs=[pl.BlockSpec(memory_space=pltpu.SMEM),
                  pl.BlockSpec((1, tq, w), lambda b, h, i: (b, i, h)),
                  pl.BlockSpec((1, S, w), lambda b, h, i: (b, 0, h)),
                  pl.BlockSpec((1, S, w), lambda b, h, i: (b, 0, h)),
                  pl.BlockSpec((1, DA_V_DIM), lambda b, h, i: (0, 0))],
        out_specs=pl.BlockSpec((1, tq, w), lambda b, h, i: (b, i, h)),
        out_shape=jax.ShapeDtypeStruct((B, S, DA_WIDTH), BF16),
        scratch_shapes=[pltpu.VMEM((2 * hb, tq, 1), F32), pltpu.VMEM((2 * hb, tq, 1), F32),
                        pltpu.VMEM((2 * hb, tq, DA_V_DIM), F32)],
        compiler_params=_cp("parallel", "parallel", "parallel"),
        name="diff_attn_prompt",
    )(lam, qb, kb, vb, g)


def _dec_attn_kernel(pt_ref, lam_ref, q_ref, kn_ref, vn_ref, g_ref, *rest, pp, lam_init):
    k_refs = rest[:pp]
    v_refs = rest[pp:2 * pp]
    o_ref = rest[2 * pp]
    m_scr, l_scr, acc_scr = rest[2 * pp + 1:]
    p_id = pl.program_id(1)
    nq = q_ref.shape[2]
    page_rows = v_refs[0].shape[2]

    @pl.when(p_id == 0)
    def _():
        m_scr[...] = jnp.full(m_scr.shape, MASK_VALUE, F32)
        l_scr[...] = jnp.zeros(l_scr.shape, F32)
        acc_scr[...] = jnp.zeros(acc_scr.shape, F32)

    q0 = q_ref[0, 0]
    q1 = q_ref[0, 1]

    def scores(k0, k1):
        return jnp.concatenate([_dot_nt(q0, k0.astype(BF16)), _dot_nt(q1, k1.astype(BF16))], axis=0)

    def same_head(shape):
        r = lax.broadcasted_iota(jnp.int32, shape, 0)
        c = lax.broadcasted_iota(jnp.int32, shape, 1)
        return (r % N_DA_HEADS) == (c % N_DA_HEADS), r, c

    def update(s, vs):
        m = m_scr[...]
        mn = jnp.maximum(m, jnp.max(s, axis=-1, keepdims=True))
        al = jnp.exp(m - mn)
        p = jnp.exp(s - mn)
        l_scr[...] = l_scr[...] * al + jnp.sum(p, axis=-1, keepdims=True)
        acc = acc_scr[...] * al
        w = s.shape[1] // len(vs)
        for i, vb in enumerate(vs):
            acc = acc + _dot(p[:, i * w:(i + 1) * w].astype(BF16), vb)
        acc_scr[...] = acc
        m_scr[...] = mn

    parts = []
    for i in range(pp):
        kr = k_refs[i].at[0, 0]
        parts.append(scores(kr[pl.ds(0, page_rows, stride=2), :], kr[pl.ds(1, page_rows, stride=2), :]))
    s = jnp.concatenate(parts, axis=1)
    s = jnp.where(same_head(s.shape)[0], s, MASK_VALUE)
    update(s, [v_refs[i][0, 0].astype(BF16) for i in range(pp)])

    @pl.when(p_id == pl.num_programs(1) - 1)
    def _():
        s = scores(kn_ref[0, 0], kn_ref[0, 1])
        same, r, c = same_head(s.shape)
        causal = (c // N_DA_HEADS) <= ((r % nq) // N_DA_HEADS)
        update(jnp.where(same & causal, s, MASK_VALUE), [vn_ref[0].astype(BF16)])
        n = acc_scr[...] / l_scr[...]
        comb = n[:nq] - lam_ref[0] * n[nq:]
        o_ref[0] = _subln(comb, g_ref[...], lam_init)


def _attn_sample(page_table, lam, q2, k_new2, v_new2, g, cache_k4, cache_v4, layer, lam_init, pp=8):
    Bd, _, nq, _ = q2.shape
    n_pages = page_table.shape[1]
    pt = page_table.reshape(-1)
    k_rows, v_rows = cache_k4.shape[2], cache_v4.shape[2]

    def page_spec(i, rows, width):
        return pl.BlockSpec((1, 1, rows, width),
                            lambda b, p, pt_ref: (layer, pt_ref[b * n_pages + p * pp + i], 0, 0))

    qk_spec = pl.BlockSpec((1, 2, nq, DA_QK_DIM), lambda b, p, pt_ref: (b, 0, 0, 0))
    row_spec = pl.BlockSpec((1, nq, DA_V_DIM), lambda b, p, pt_ref: (b, 0, 0))
    grid_spec = pltpu.PrefetchScalarGridSpec(
        num_scalar_prefetch=1,
        grid=(Bd, n_pages // pp),
        in_specs=[pl.BlockSpec(memory_space=pltpu.SMEM), qk_spec, qk_spec, row_spec,
                  pl.BlockSpec((1, DA_V_DIM), lambda b, p, pt_ref: (0, 0))]
                 + [page_spec(i, k_rows, DA_QK_DIM) for i in range(pp)]
                 + [page_spec(i, v_rows, DA_V_DIM) for i in range(pp)],
        out_specs=row_spec,
        scratch_shapes=[pltpu.VMEM((2 * nq, 1), F32), pltpu.VMEM((2 * nq, 1), F32),
                        pltpu.VMEM((2 * nq, DA_V_DIM), F32)],
    )
    return pl.pallas_call(
        functools.partial(_dec_attn_kernel, pp=pp, lam_init=lam_init),
        grid_spec=grid_spec,
        out_shape=jax.ShapeDtypeStruct((Bd, nq, DA_V_DIM), F32),
        compiler_params=_cp("parallel", "arbitrary"),
        name="diff_attn_decode",
    )(pt, lam, q2, k_new2, v_new2, g, *([cache_k4] * pp), *([cache_v4] * pp))


def _rw_prep_kernel(h_ref, sh0_ref, mu_ref, w0_ref, a0_ref, wup_ref, aup_ref, gup_ref,
                    r_ref, k_ref, v_ref, lw_ref, a_ref, g_ref, prev_scr):
    cp = h_ref.shape[1]

    @pl.when(pl.program_id(1) == 0)
    def _():
        prev_scr[...] = sh0_ref[0]

    rw = h_ref[0]
    rowi = lax.broadcasted_iota(jnp.int32, rw.shape, 0)
    prev = jnp.where(rowi == 0, prev_scr[...], pltpu.roll(rw, 1, 0))
    prev_scr[...] = rw[cp - 1:cp, :]
    xm = rw + (prev - rw) * mu_ref[...]
    c1, c2, c3 = RW_WIDTH, 2 * RW_WIDTH, 3 * RW_WIDTH
    r_ref[0] = xm[:, :c1]
    k_ref[0] = xm[:, c1:c2]
    v_ref[0] = xm[:, c2:c3]
    lo = xm[:, c3:c3 + LANES]
    dg = xm[:, c3 + LANES:]
    z = -(w0_ref[...] + _dot(jnp.tanh(lo), wup_ref[...], HIGHEST))
    softplus = jnp.maximum(z, 0.0) + jnp.log1p(jnp.exp(-jnp.abs(z)))
    lw_ref[0] = -jnp.exp(-softplus - 0.5)
    a_ref[0] = jax.nn.sigmoid(a0_ref[...] + _dot(lo, aup_ref[...], HIGHEST))
    g_ref[0] = _dot(jax.nn.sigmoid(dg), gup_ref[...], HIGHEST)


def _rw_prep(hrw3, sh0, mu, w0, a0, wup, aup, gup):
    B, T, _ = hrw3.shape
    cp = min(T, 256)
    full = lambda shape: pl.BlockSpec(shape, lambda b, c: (0,) * len(shape))
    out_blk = pl.BlockSpec((1, cp, RW_WIDTH), lambda b, c: (b, c, 0))
    out_sds = jax.ShapeDtypeStruct((B, T, RW_WIDTH), F32)
    return pl.pallas_call(
        _rw_prep_kernel,
        grid=(B, T // cp),
        in_specs=[pl.BlockSpec((1, cp, N_RW_PAD), lambda b, c: (b, c, 0)),
                  pl.BlockSpec((1, 1, N_RW_PAD), lambda b, c: (b, 0, 0)),
                  full((1, N_RW_PAD)), full((1, RW_WIDTH)), full((1, RW_WIDTH)),
                  full((LANES, RW_WIDTH)), full((LANES, RW_WIDTH)), full((G_PAD, RW_WIDTH))],
        out_specs=[out_blk] * 6,
        out_shape=[out_sds] * 6,
        scratch_shapes=[pltpu.VMEM((1, N_RW_PAD), F32)],
        compiler_params=_cp("parallel", "arbitrary"),
        name="rwkv_prep",
    )(hrw3, sh0, mu, w0, a0, wup, aup, gup)


def _split(x):
    hi = x.astype(BF16)
    lo = (x - hi.astype(F32)).astype(BF16)
    return hi, lo


def _mm3(a, b, dot=_dot):
    return dot(a[1], b[0]) + dot(a[0], b[1]) + dot(a[0], b[0])


def _mm_exact_rhs(x, m):
    hi, lo = _split(x)
    return _dot(lo, m) + _dot(hi, m)


def _rw_chunk(args, masks, t_valid):
    C = args[0][0].shape[0]
    incl_b, strict, incl, eye, seg_b, head_a, blockdiag, diag, levels = masks
    head_masks = (head_a, jnp.logical_not(head_a))
    P = range(len(args))
    PH = [(p, hd) for p in P for hd in range(2)]

    pre = []
    for (r, k, v, lw, a, g, kkp, kap, rkp, lnw, lnb, H) in args:
        if t_valid < C:
            valid = lax.broadcasted_iota(jnp.int32, (C, LANES), 0) < t_valid
            lw = jnp.where(valid, lw, 0.0)
            k = jnp.where(valid, k, 0.0)
            v = jnp.where(valid, v, 0.0)
        kk = k * kkp
        kk = kk * lax.rsqrt(jnp.maximum(_mm_exact_rhs(kk * kk, seg_b), 1e-24))
        k2 = k * (1.0 + (a - 1.0) * kap)
        b = kk * a
        lw_hi, lw_lo = _split(lw)
        lw_lo2 = (lw - lw_hi.astype(F32) - lw_lo.astype(F32)).astype(BF16)
        gam = _dot(incl_b, lw_lo2) + _dot(incl_b, lw_lo) + _dot(incl_b, lw_hi)
        gl = gam[C - 1:C, :]
        rt = r * jnp.exp(gam)
        at = -kk * jnp.exp(gam - lw)
        ginv = jnp.exp(-gam)
        gh = jnp.exp(gl - gam)
        pre.append(dict(r=r, v=v, k2=k2, g=g, rkp=rkp, lnw=lnw, lnb=lnb, H=H, gl=gl, rt=rt, at=at,
                        bt=_split(b * ginv), kt=_split(k2 * ginv), bh=b * gh, kh=k2 * gh,
                        Hs=_split(H), vs=_split(v)))

    stacked = C % LANES == 0
    cat0 = lambda xs: jnp.concatenate(xs, axis=0)
    if stacked:
        ar = [cat0([pre[p]['at'], pre[p]['rt']]) for p in P]
        bk = [tuple(cat0([pre[p]['bt'][i], pre[p]['kt'][i]]) for i in range(2)) for p in P]
        head2 = [cat0([hm, hm]) for hm in head_masks]
        A = {(p, hd): _mm3(_split(jnp.where(head2[hd], ar[p], 0.0)), bk[p], _dot_nt) for p, hd in PH}
        aab = {c: jnp.where(strict, A[c][:C, :C], 0.0) for c in PH}
        aak = {c: jnp.where(strict, A[c][:C, C:], 0.0) for c in PH}
        arb = {c: jnp.where(incl, A[c][C:, :C], 0.0) for c in PH}
        ark = {c: jnp.where(incl, A[c][C:, C:], 0.0) for c in PH}
        XR = [_mm3(_split(ar[p]), pre[p]['Hs']) for p in P]
        X = [xr[:C] for xr in XR]
        RH = [xr[C:] for xr in XR]
    else:
        atm = {(p, hd): _split(jnp.where(head_masks[hd], pre[p]['at'], 0.0)) for p, hd in PH}
        rtm = {(p, hd): _split(jnp.where(head_masks[hd], pre[p]['rt'], 0.0)) for p, hd in PH}
        aab = {c: jnp.where(strict, _mm3(atm[c], pre[c[0]]['bt'], _dot_nt), 0.0) for c in PH}
        aak = {c: jnp.where(strict, _mm3(atm[c], pre[c[0]]['kt'], _dot_nt), 0.0) for c in PH}
        arb = {c: jnp.where(incl, _mm3(rtm[c], pre[c[0]]['bt'], _dot_nt), 0.0) for c in PH}
        ark = {c: jnp.where(incl, _mm3(rtm[c], pre[c[0]]['kt'], _dot_nt), 0.0) for c in PH}
        X = [_mm3(_split(pre[p]['at']), pre[p]['Hs']) for p in P]
        RH = [_mm3(_split(pre[p]['rt']), pre[p]['Hs']) for p in P]
    tinv = {c: eye + jnp.where(levels[0], aab[c], 0.0) for c in PH}
    for lvl in levels[1:]:
        ts = {c: _split(tinv[c]) for c in PH}
        mid = {c: _split(_mm3(_split(jnp.where(lvl, aab[c], 0.0)), ts[c])) for c in PH}
        tinv = {c: tinv[c] + _mm3(ts[c], mid[c]) for c in PH}
    if stacked:
        ys2 = [_mm3(_split(cat0([aak[(p, 0)], aak[(p, 1)]])), pre[p]['vs']) for p in P]
        ysum = [_split(X[p] + jnp.where(head_a, ys2[p][:C], ys2[p][C:])) for p in P]
        u2 = [_mm3(_split(cat0([tinv[(p, 0)], tinv[(p, 1)]])), ysum[p]) for p in P]
        U = [jnp.where(head_a, u2[p][:C], u2[p][C:]) for p in P]
    else:
        ys = {c: _mm3(_split(aak[c]), pre[c[0]]['vs']) for c in PH}
        ysum = [_split(X[p] + jnp.where(head_a, ys[(p, 0)], ys[(p, 1)])) for p in P]
        u = {c: _mm3(_split(tinv[c]), ysum[c[0]]) for c in PH}
        U = [jnp.where(head_a, u[(p, 0)], u[(p, 1)]) for p in P]

    hn, uvh = [], []
    for p in P:
        dg = jnp.where(diag, jnp.exp(pre[p]['gl']), 0.0)
        lhs = _split(cat0([pre[p]['bh'], pre[p]['kh'], dg]))
        uvh.append(_split(cat0([U[p], pre[p]['v'], pre[p]['H']])))
        hn.append(jnp.where(blockdiag, _mm3(lhs, uvh[p], _dot_tn), 0.0))

    if stacked:
        both = lambda p, hd: jnp.concatenate([arb[(p, hd)], ark[(p, hd)]], axis=1)
        dd2 = [_mm3(_split(cat0([both(p, 0), both(p, 1)])), tuple(t[:2 * C] for t in uvh[p])) for p in P]
        dsel = [jnp.where(head_a, dd2[p][:C], dd2[p][C:]) for p in P]
    else:
        Us = [_split(U[p]) for p in P]
        dd = {c: _mm3(_split(arb[c]), Us[c[0]]) + _mm3(_split(ark[c]), pre[c[0]]['vs']) for c in PH}
        dsel = [jnp.where(head_a, dd[(p, 0)], dd[(p, 1)]) for p in P]
    inv_n = 1.0 / RW_HEAD
    outs = []
    for p in P:
        q = pre[p]
        O = RH[p] + dsel[p]
        mean = _mm_exact_rhs(O, seg_b) * inv_n
        d = O - mean
        var = _mm_exact_rhs(d * d, seg_b) * inv_n
        on = d * lax.rsqrt(var + LNX_EPS) * q['lnw'] + q['lnb']
        bonus = _mm_exact_rhs(q['r'] * q['k2'] * q['rkp'], seg_b) * q['v']
        outs.append(((on + bonus) * q['g'], hn[p]))
    return outs


def _rw_scan_kernel(r_ref, k_ref, v_ref, lw_ref, a_ref, g_ref, kk_ref, ka_ref, rk_ref,
                    lnw_ref, lnb_ref, h0_ref, o_ref, hout_ref, h_scr, *, t_valid):
    C = r_ref.shape[1]
    n_pairs = r_ref.shape[2] // LANES
    ci = pl.program_id(2)

    @pl.when(ci == 0)
    def _():
        h_scr[...] = h0_ref[0]

    ti = lax.broadcasted_iota(jnp.int32, (C, C), 0)
    si = lax.broadcasted_iota(jnp.int32, (C, C), 1)
    hr = lax.broadcasted_iota(jnp.int32, (LANES, LANES), 0)
    hc = lax.broadcasted_iota(jnp.int32, (LANES, LANES), 1)
    blockdiag = (hr // RW_HEAD) == (hc // RW_HEAD)
    levels = []
    m = 1
    while m < C:
        levels.append(((ti // (2 * m)) == (si // (2 * m))) & ((ti % (2 * m)) >= m) & ((si % (2 * m)) < m))
        m *= 2
    masks = ((si <= ti).astype(BF16), si < ti, si <= ti, (si == ti).astype(F32),
             blockdiag.astype(BF16),
             lax.broadcasted_iota(jnp.int32, (C, LANES), 1) < RW_HEAD, blockdiag, hr == hc, levels)
    sls = [slice(LANES * pi, LANES * (pi + 1)) for pi in range(n_pairs)]
    args = [(r_ref[0, :, sl], k_ref[0, :, sl], v_ref[0, :, sl], lw_ref[0, :, sl], a_ref[0, :, sl],
             g_ref[0, :, sl], kk_ref[:, sl], ka_ref[:, sl], rk_ref[:, sl], lnw_ref[:, sl], lnb_ref[:, sl],
             h_scr[pi]) for pi, sl in enumerate(sls)]
    res = _rw_chunk(args, masks, t_valid)
    for pi, sl in enumerate(sls):
        o_ref[0, :, sl] = res[pi][0].astype(o_ref.dtype)
        h_scr[pi] = res[pi][1]

    @pl.when(ci == pl.num_programs(2) - 1)
    def _():
        hout_ref[0] = h_scr[...]


def _rw_scan(r, k, v, lw, a, g, kk, ka, rk, lnw, lnb, h0, t_valid, out_dtype):
    B, T, _ = r.shape
    C = min(T, 128)
    pb = 4 if C == LANES else N_RW_PAIRS
    w = LANES * pb
    tok = pl.BlockSpec((1, C, w), lambda b, p, c: (b, c, p))
    vec = pl.BlockSpec((1, w), lambda b, p, c: (0, p))
    st = pl.BlockSpec((1, pb, LANES, LANES), lambda b, p, c: (b, p, 0, 0))
    return pl.pallas_call(
        functools.partial(_rw_scan_kernel, t_valid=t_valid),
        grid=(B, N_RW_PAIRS // pb, T // C),
        in_specs=[tok] * 6 + [vec] * 5 + [st],
        out_specs=[tok, st],
        out_shape=[jax.ShapeDtypeStruct((B, T, RW_WIDTH), out_dtype),
                   jax.ShapeDtypeStruct((B, N_RW_PAIRS, LANES, LANES), F32)],
        scratch_shapes=[pltpu.VMEM((pb, LANES, LANES), F32)],
        compiler_params=_cp("parallel", "parallel", "arbitrary"),
        name="rwkv_scan",
    )(r, k, v, lw, a, g, kk, ka, rk, lnw, lnb, h0)


def _out_proj_kernel(a1_ref, a2_ref, w_ref, res_ref, o_ref):
    acc = _dot(a1_ref[...].astype(BF16), w_ref[:DA_WIDTH, :])
    acc = acc + _dot(a2_ref[...].astype(BF16), w_ref[DA_WIDTH:, :])
    o_ref[...] = res_ref[...] + acc


def _out_proj(a1, a2, w, res):
    M, D = res.shape
    tm = min(M, 1024)
    tn = 512
    return pl.pallas_call(
        _out_proj_kernel,
        grid=(M // tm, D // tn),
        in_specs=[pl.BlockSpec((tm, DA_WIDTH), lambda i, j: (i, 0)),
                  pl.BlockSpec((tm, RW_WIDTH), lambda i, j: (i, 0)),
                  pl.BlockSpec((DA_WIDTH + RW_WIDTH, tn), lambda i, j: (0, j)),
                  pl.BlockSpec((tm, tn), lambda i, j: (i, j))],
        out_specs=pl.BlockSpec((tm, tn), lambda i, j: (i, j)),
        out_shape=jax.ShapeDtypeStruct((M, D), F32),
        compiler_params=_cp("parallel", "parallel"),
        name="out_proj",
    )(a1, a2, w, res)


def _ffn_up_kernel(x_ref, g_ref, wg_ref, wu_ref, o_ref, xn_ref):
    @pl.when(pl.program_id(1) == 0)
    def _():
        x = x_ref[...]
        ms = jnp.mean(x * x, axis=-1, keepdims=True)
        xn_ref[...] = (x * lax.rsqrt(ms + NORM_EPS) * g_ref[...]).astype(BF16)

    xn = xn_ref[...]
    gate = _dot(xn, wg_ref[...])
    up = _dot(xn, wu_ref[...])
    o_ref[...] = (gate * jax.nn.sigmoid(gate) * up).astype(o_ref.dtype)


def _ffn_up(x, g, wg, wu):
    M, D = x.shape
    tm = min(M, 1024)
    tn = 512
    return pl.pallas_call(
        _ffn_up_kernel,
        grid=(M // tm, D_FF // tn),
        in_specs=[pl.BlockSpec((tm, D), lambda i, j: (i, 0)),
                  pl.BlockSpec((1, D), lambda i, j: (0, 0)),
                  pl.BlockSpec((D, tn), lambda i, j: (0, j)),
                  pl.BlockSpec((D, tn), lambda i, j: (0, j))],
        out_specs=pl.BlockSpec((tm, tn), lambda i, j: (i, j)),
        out_shape=jax.ShapeDtypeStruct((M, D_FF), BF16),
        scratch_shapes=[pltpu.VMEM((tm, D), BF16)],
        compiler_params=_cp("parallel", "arbitrary"),
        name="ffn_gate_up",
    )(x, g, wg, wu)


def _ffn_down_kernel(a_ref, w_ref, res_ref, o_ref):
    o_ref[...] = res_ref[...] + _dot(a_ref[...], w_ref[...])


def _ffn_down(a, w, res):
    M, D = res.shape
    tm = min(M, 1024)
    tn = 512
    return pl.pallas_call(
        _ffn_down_kernel,
        grid=(M // tm, D // tn),
        in_specs=[pl.BlockSpec((tm, D_FF), lambda i, j: (i, 0)),
                  pl.BlockSpec((D_FF, tn), lambda i, j: (0, j)),
                  pl.BlockSpec((tm, tn), lambda i, j: (i, j))],
        out_specs=pl.BlockSpec((tm, tn), lambda i, j: (i, j)),
        out_shape=jax.ShapeDtypeStruct((M, D), F32),
        compiler_params=_cp("parallel", "parallel"),
        name="ffn_down",
    )(a, w, res)


def _rope_tables(pos):
    half = DA_QK_DIM // 2
    inv_freq = jnp.power(ROPE_THETA, -jnp.arange(half, dtype=F32) / half)
    ang = pos.astype(F32)[:, None] * inv_freq[None, :]
    cos = jnp.cos(ang)
    sin = jnp.sin(ang)
    reps = LANES // DA_QK_DIM
    cos_t = jnp.tile(jnp.concatenate([cos, cos], axis=-1), (1, reps))
    sin_t = jnp.tile(jnp.concatenate([-sin, sin], axis=-1), (1, reps))
    return cos_t, sin_t


def _state_to_pairs(S):
    B = S.shape[0]
    St = jnp.swapaxes(S, -1, -2).reshape(B, N_RW_PAIRS, 2, RW_HEAD, RW_HEAD)
    eye2 = jnp.eye(2, dtype=S.dtype)
    Hp = St[:, :, :, :, None, :] * eye2[None, None, :, None, :, None]
    return Hp.reshape(B, N_RW_PAIRS, LANES, LANES)


def _pairs_to_state(Hp):
    B = Hp.shape[0]
    H5 = Hp.reshape(B, N_RW_PAIRS, 2, RW_HEAD, 2, RW_HEAD)
    diag = jnp.stack([H5[:, :, 0, :, 0, :], H5[:, :, 1, :, 1, :]], axis=2)
    return jnp.swapaxes(diag.reshape(B, N_RW_HEADS, RW_HEAD, RW_HEAD), -1, -2)


def _split_maps(x, B, T):
    x5 = x.reshape(B, T, N_DA_HEADS, 2, DA_QK_DIM)
    return jnp.transpose(x5, (0, 3, 1, 2, 4)).reshape(B, 2, T * N_DA_HEADS, DA_QK_DIM)


def _pad_rows(x, n):
    return jnp.pad(x, ((0, 0), (0, n - x.shape[1]), (0, 0)))


def _layer(x3, cos, sin, S0, shift0, lam, lam_init, p, decode=None):
    B, T, D = x3.shape
    M = B * T
    x = x3.reshape(M, D)
    hda = _norm_mm(x, p['norm_mix'], p['w_da'], 1024)
    hrw = _norm_mm(x, p['norm_mix'], p['w_rw'], 1152)
    qb, kf, kb, vb = _qk_post(hda, p['q_gain'], p['k_gain'], cos, sin, T)
    hda3 = hda.reshape(B, T, 3 * DA_WIDTH)
    v_new = hda3[:, :, 2 * DA_WIDTH:]
    if decode is None:
        o_da = _attn_prompt(lam, qb.reshape(B, T, DA_WIDTH), kb.reshape(B, T, DA_WIDTH),
                            vb.reshape(B, T, DA_WIDTH), p['subln_gain'], lam_init)
    else:
        page_table, cache_k4, cache_v4, layer = decode
        o_da = _attn_sample(page_table, lam, _split_maps(qb, B, T), _split_maps(kf, B, T),
                            v_new.reshape(B, T * N_DA_HEADS, DA_V_DIM), p['subln_gain'], cache_k4, cache_v4,
                            layer, lam_init)
    hrw3 = hrw.reshape(B, T, N_RW_PAD)
    shift_new = hrw3[:, -1, :N_RW_COLS]
    t_pad = max(T, 8)
    hrw3p = _pad_rows(hrw3, t_pad)
    sh0 = jnp.pad(shift0, ((0, 0), (0, N_RW_PAD - N_RW_COLS)))[:, None, :]
    r, k, v, lw, a, g = _rw_prep(hrw3p, sh0, p['mu_shift'], p['w0'], p['a0'], p['w_up'], p['a_up'],
                                 p['g_up'])
    o_rw, hp = _rw_scan(r, k, v, lw, a, g, p['k_k'], p['k_a'], p['r_k'], p['lnx_w'], p['lnx_b'],
                        _state_to_pairs(S0), T, BF16 if decode is None else F32)
    o_rw = o_rw[:, :T]
    S_new = _pairs_to_state(hp)
    x1 = _out_proj(o_da.reshape(M, DA_WIDTH), o_rw.reshape(M, RW_WIDTH), p['w_out'], x)
    ff = _ffn_up(x1, p['norm_ffn'], p['w_gate'], p['w_up_ffn'])
    x2 = _ffn_down(ff, p['w_down'], x1)
    return x2.reshape(B, T, D), kf, v_new, S_new, shift_new


def kernel(x_prompt, x_sample, cache_k, cache_v, state_wkv, state_shift, page_table, norm_mix, w_in, q_gain, k_gain, lambda_q1, lambda_k1, lambda_q2, lambda_k2, subln_gain, mu_shift, w0, w_up, a0, a_up, g_up, k_k, k_a, r_k, lnx_w, lnx_b, w_out, norm_ffn, w_gate, w_up_ffn, w_down):
    B, S, _ = x_prompt.shape
    Bd, T, _ = x_sample.shape
    depth = w_in.shape[0]
    n_pages = page_table.shape[1]
    past_len = n_pages * PAGE_SIZE
    cos_p, sin_p = _rope_tables(jnp.arange(S, dtype=jnp.int32))
    cos_s, sin_s = _rope_tables(past_len + jnp.arange(T, dtype=jnp.int32))
    cos_s = jnp.tile(cos_s, (Bd, 1))
    sin_s = jnp.tile(sin_s, (Bd, 1))
    n_pool = cache_k.shape[1]
    cache_k4 = cache_k.reshape(depth, n_pool, PAGE_SIZE * N_DA_SUB, DA_QK_DIM)
    cache_v4 = cache_v.reshape(depth, n_pool, PAGE_SIZE * N_DA_HEADS, DA_V_DIM)
    row = lambda t: t.reshape(1, -1)
    tile_gain = lambda t: jnp.tile(t, LANES // DA_QK_DIM).reshape(1, LANES)

    xp, xs = x_prompt, x_sample
    outs = [[] for _ in range(8)]
    for l in range(depth):
        lam_init = 0.8 - 0.6 * math.exp(-0.3 * l)
        lam = (jnp.exp(jnp.sum(lambda_q1[l] * lambda_k1[l])) - jnp.exp(jnp.sum(lambda_q2[l] * lambda_k2[l]))
               + lam_init).reshape(1).astype(F32)
        w_in_b = w_in[l].astype(BF16)
        p = {
            'norm_mix': row(norm_mix[l]),
            'w_da': w_in_b[:, :3 * DA_WIDTH],
            'w_rw': jnp.pad(w_in_b[:, 3 * DA_WIDTH:], ((0, 0), (0, N_RW_PAD - N_RW_COLS))),
            'q_gain': tile_gain(q_gain[l]), 'k_gain': tile_gain(k_gain[l]),
            'subln_gain': row(subln_gain[l]),
            'mu_shift': jnp.pad(row(mu_shift[l]), ((0, 0), (0, N_RW_PAD - N_RW_COLS))),
            'w0': row(w0[l]), 'a0': row(a0[l]),
            'w_up': jnp.pad(w_up[l], ((0, LANES - W_LORA), (0, 0))),
            'a_up': jnp.pad(a_up[l], ((W_LORA, LANES - W_LORA - A_LORA), (0, 0))),
            'g_up': jnp.pad(g_up[l], ((0, G_PAD - G_LORA), (0, 0))),
            'k_k': row(k_k[l]), 'k_a': row(k_a[l]), 'r_k': row(r_k[l]),
            'lnx_w': row(lnx_w[l]), 'lnx_b': row(lnx_b[l]),
            'w_out': w_out[l].astype(BF16), 'norm_ffn': row(norm_ffn[l]),
            'w_gate': w_gate[l].astype(BF16), 'w_up_ffn': w_up_ffn[l].astype(BF16),
            'w_down': w_down[l].astype(BF16),
        }
        S0_p = jnp.zeros((B, N_RW_HEADS, RW_HEAD, RW_HEAD), F32)
        sh0_p = jnp.zeros((B, N_RW_COLS), F32)
        xp, k_new, v_new, S_new, sh_new = _layer(xp, cos_p, sin_p, S0_p, sh0_p, lam, lam_init, p)
        outs[0].append(k_new.reshape(B, S // PAGE_SIZE, PAGE_SIZE, N_DA_SUB, DA_QK_DIM))
        outs[1].append(v_new.reshape(B, S // PAGE_SIZE, PAGE_SIZE, N_DA_HEADS, DA_V_DIM))
        outs[2].append(S_new)
        outs[3].append(sh_new)
        xs, k_new, v_new, S_new, sh_new = _layer(xs, cos_s, sin_s, state_wkv[l], state_shift[l], lam,
                                                 lam_init, p, decode=(page_table, cache_k4, cache_v4, l))
        outs[4].append(k_new.reshape(Bd, T, N_DA_SUB, DA_QK_DIM))
        outs[5].append(v_new.reshape(Bd, T, N_DA_HEADS, DA_V_DIM))
        outs[6].append(S_new)
        outs[7].append(sh_new)
    return (xp, xs) + tuple(jnp.stack(o) for o in outs)
```

```python
import functools
import math

import jax
import jax.numpy as jnp
from jax import lax
from jax.experimental import pallas as pl
from jax.experimental.pallas import tpu as pltpu

F32 = jnp.float32
BF16 = jnp.bfloat16

D_MODEL = 2048
PAGE_SIZE = 128
DA_WIDTH = 1024
RW_WIDTH = 1024
DA_V_DIM = 128
N_DA_HEADS = 8
DA_QK_DIM = 64
N_DA_SUB = 16
RW_HEAD = 64
N_RW_HEADS = 16
N_RW_PAIRS = N_RW_HEADS // 2
W_LORA = 64
A_LORA = 64
G_LORA = 160
N_RW_COLS = 3 * RW_WIDTH + W_LORA + A_LORA + G_LORA
LANES = 128
N_RW_PAD = ((N_RW_COLS + LANES - 1) // LANES) * LANES
G_PAD = N_RW_PAD - 3 * RW_WIDTH - LANES
D_FF = 5632
ROPE_THETA = 10000.0
NORM_EPS = 1e-6
LNX_EPS = 64e-5
MASK_VALUE = -1e30
DA_SCALE = DA_QK_DIM ** -0.5
VMEM_LIMIT_BYTES = 56 * 1024 * 1024
NEW_PAD = 16


def _cp(*sem):
    return pltpu.CompilerParams(dimension_semantics=sem, vmem_limit_bytes=VMEM_LIMIT_BYTES)


def _dot(a, b, precision=None):
    return jnp.dot(a, b, preferred_element_type=F32, precision=precision)


def _dot_nt(a, b, precision=None):
    return lax.dot_general(a, b, (((1,), (1,)), ((), ())), preferred_element_type=F32,
                           precision=precision)


def _dot_tn(a, b, precision=None):
    return lax.dot_general(a, b, (((0,), (0,)), ((), ())), preferred_element_type=F32,
                           precision=precision)


def _seg_ones(n, seg):
    r = lax.broadcasted_iota(jnp.int32, (n, n), 0) // seg
    c = lax.broadcasted_iota(jnp.int32, (n, n), 1) // seg
    return (r == c).astype(F32)


def _norm_mm_kernel(x_ref, g_ref, w_ref, o_ref, xn_ref):
    @pl.when(pl.program_id(1) == 0)
    def _():
        x = x_ref[...]
        ms = jnp.mean(x * x, axis=-1, keepdims=True)
        xn_ref[...] = (x * lax.rsqrt(ms + NORM_EPS) * g_ref[...]).astype(BF16)

    o_ref[...] = _dot(xn_ref[...], w_ref[...])


def _norm_mm(x, g, w, tn):
    M, D = x.shape
    N = w.shape[1]
    tm = min(M, 1024)
    return pl.pallas_call(
        _norm_mm_kernel,
        grid=(M // tm, N // tn),
        in_specs=[pl.BlockSpec((tm, D), lambda i, j: (i, 0)),
                  pl.BlockSpec((1, D), lambda i, j: (0, 0)),
                  pl.BlockSpec((D, tn), lambda i, j: (0, j))],
        out_specs=pl.BlockSpec((tm, tn), lambda i, j: (i, j)),
        out_shape=jax.ShapeDtypeStruct((M, N), F32),
        scratch_shapes=[pltpu.VMEM((tm, D), BF16)],
        compiler_params=_cp("parallel", "arbitrary"),
        name="norm_in_proj",
    )(x, g, w)


def _qk_post_kernel(hq_ref, hk_ref, hv_ref, qg_ref, kg_ref, cos_ref, sin_ref, qb_ref, kf_ref, kb_ref, vb_ref):
    tm = hq_ref.shape[0]
    vb_ref[...] = hv_ref[...].astype(BF16)
    lane = lax.broadcasted_iota(jnp.int32, (tm, LANES), 1)
    first = (lane % DA_QK_DIM) < (DA_QK_DIM // 2)
    seg = (_seg_ones(LANES, DA_QK_DIM) * (1.0 / DA_QK_DIM)).astype(BF16)
    cos = cos_ref[...]
    sin = sin_ref[...]

    def proc(x, g):
        ms = _mm_exact_rhs(x * x, seg)
        y = x * lax.rsqrt(ms + NORM_EPS) * g
        partner = jnp.where(first, pltpu.roll(y, LANES - DA_QK_DIM // 2, 1),
                            pltpu.roll(y, DA_QK_DIM // 2, 1))
        return y * cos + partner * sin

    for c in range(DA_WIDTH // LANES):
        sl = slice(LANES * c, LANES * (c + 1))
        q = proc(hq_ref[:, sl], qg_ref[...])
        qb_ref[:, sl] = (q * DA_SCALE).astype(BF16)
        k = proc(hk_ref[:, sl], kg_ref[...])
        kf_ref[:, sl] = k
        kb_ref[:, sl] = k.astype(BF16)


def _qk_post(hda, qg, kg, cos, sin, T):
    M = hda.shape[0]
    tm = min(M, 512)
    if cos.shape[0] == M:
        tab_map = lambda i: (i, 0)
    else:
        nper = T // tm
        tab_map = lambda i: (i % nper, 0)
    blk = lambda c: pl.BlockSpec((tm, DA_WIDTH), lambda i: (i, c))
    vec = pl.BlockSpec((1, LANES), lambda i: (0, 0))
    tab = pl.BlockSpec((tm, LANES), tab_map)
    return pl.pallas_call(
        _qk_post_kernel,
        grid=(M // tm,),
        in_specs=[blk(0), blk(1), blk(2), vec, vec, tab, tab],
        out_specs=[blk(0), blk(0), blk(0), blk(0)],
        out_shape=[jax.ShapeDtypeStruct((M, DA_WIDTH), BF16),
                   jax.ShapeDtypeStruct((M, DA_WIDTH), F32),
                   jax.ShapeDtypeStruct((M, DA_WIDTH), BF16),
                   jax.ShapeDtypeStruct((M, DA_WIDTH), BF16)],
        compiler_params=_cp("parallel"),
        name="qk_norm_rope",
    )(hda, hda, hda, qg, kg, cos, sin)


def _subln(o, g, lam_init):
    ms = jnp.mean(o * o, axis=-1, keepdims=True)
    return o * lax.rsqrt(ms + NORM_EPS) * g * (1.0 - lam_init)


def _attn_kernel(lam_ref, q_ref, k_ref, v_ref, g_ref, o_ref, m_scr, l_scr, acc_scr, *, tq, tk, lam_init):
    qi = pl.program_id(2)
    n_heads = q_ref.shape[2] // LANES
    lane = lax.broadcasted_iota(jnp.int32, (tq, LANES), 1)
    chains = [(hd, mi) for hd in range(n_heads) for mi in range(2)]
    qs = []
    for hd, mi in chains:
        q = q_ref[0, :, LANES * hd:LANES * (hd + 1)]
        keep = (lane < DA_QK_DIM) if mi == 0 else (lane >= DA_QK_DIM)
        qs.append(jnp.where(keep, q, jnp.zeros_like(q)))
    m_scr[...] = jnp.full(m_scr.shape, MASK_VALUE, F32)
    l_scr[...] = jnp.zeros(l_scr.shape, F32)
    acc_scr[...] = jnp.zeros(acc_scr.shape, F32)

    def block(j, masked):
        off = pl.multiple_of(j * tk, tk)
        kbs = [k_ref[0, pl.ds(off, tk), LANES * hd:LANES * (hd + 1)] for hd in range(n_heads)]
        vbs = [v_ref[0, pl.ds(off, tk), LANES * hd:LANES * (hd + 1)] for hd in range(n_heads)]
        nc = range(len(chains))
        ms = [m_scr[c] for c in nc]
        ls = [l_scr[c] for c in nc]
        accs = [acc_scr[c] for c in nc]
        ss = [_dot_nt(qs[c], kbs[chains[c][0]]) for c in nc]
        if masked:
            row = qi * tq + lax.broadcasted_iota(jnp.int32, (tq, tk), 0)
            col = off + lax.broadcasted_iota(jnp.int32, (tq, tk), 1)
            ss = [jnp.where(col <= row, s, MASK_VALUE) for s in ss]
        mns = [jnp.maximum(ms[c], jnp.max(ss[c], axis=-1, keepdims=True)) for c in nc]
        als = [jnp.exp(ms[c] - mns[c]) for c in nc]
        ps = [jnp.exp(ss[c] - mns[c]) for c in nc]
        sums = [jnp.sum(ps[c], axis=-1, keepdims=True) for c in nc]
        pvs = [_dot(ps[c].astype(BF16), vbs[chains[c][0]]) for c in nc]
        for c in nc:
            m_scr[c] = mns[c]
            l_scr[c] = ls[c] * als[c] + sums[c]
            acc_scr[c] = accs[c] * als[c] + pvs[c]

    n_full = (qi * tq) // tk

    def body(j, c):
        block(j, False)
        return c

    lax.fori_loop(0, n_full, body, 0)
    block(n_full, True)
    for hd in range(n_heads):
        c1, c2 = 2 * hd, 2 * hd + 1
        o = acc_scr[c1] / l_scr[c1] - lam_ref[0] * (acc_scr[c2] / l_scr[c2])
        o_ref[0, :, LANES * hd:LANES * (hd + 1)] = _subln(o, g_ref[...], lam_init).astype(o_ref.dtype)


def _attn_prompt(lam, qb, kb, vb, g, lam_init):
    B, S, _ = qb.shape
    tq = min(S, 128)
    tk = min(S, 256)
    hb = 2
    w = LANES * hb
    return pl.pallas_call(
        functools.partial(_attn_kernel, tq=tq, tk=tk, lam_init=lam_init),
        grid=(B, N_DA_HEADS // hb, S // tq),
        in_specs=[pl.BlockSpec(memory_space=pltpu.SMEM),
                  pl.BlockSpec((1, tq, w), lambda b, h, i: (b, i, h)),
                  pl.BlockSpec((1, S, w), lambda b, h, i: (b, 0, h)),
                  pl.BlockSpec((1, S, w), lambda b, h, i: (b, 0, h)),
                  pl.BlockSpec((1, DA_V_DIM), lambda b, h, i: (0, 0))],
        out_specs=pl.BlockSpec((1, tq, w), lambda b, h, i: (b, i, h)),
        out_shape=jax.ShapeDtypeStruct((B, S, DA_WIDTH), BF16),
        scratch_shapes=[pltpu.VMEM((2 * hb, tq, 1), F32), pltpu.VMEM((2 * hb, tq, 1), F32),
                        pltpu.VMEM((2 * hb, tq, DA_V_DIM), F32)],
        compiler_params=_cp("parallel", "parallel", "parallel"),
        name="diff_attn_prompt",
    )(lam, qb, kb, vb, g)


def _dec_attn_kernel(pt_ref, lam_ref, q_ref, kn_ref, vn_ref, g_ref, *rest, pp, n_new, lam_init):
    k_refs = rest[:pp]
    v_refs = rest[pp:2 * pp]
    o_ref = rest[2 * pp]
    m_scr, l_scr, acc_scr = rest[2 * pp + 1:]
    p_id = pl.program_id(1)
    rows_h = 2 * n_new

    @pl.when(p_id == 0)
    def _():
        m_scr[...] = jnp.full(m_scr.shape, MASK_VALUE, F32)
        l_scr[...] = jnp.zeros(l_scr.shape, F32)
        acc_scr[...] = jnp.zeros(acc_scr.shape, F32)

    q = q_ref[0]

    def update(s, v_of_head):
        m = m_scr[...]
        mn = jnp.maximum(m, jnp.max(s, axis=-1, keepdims=True))
        al = jnp.exp(m - mn)
        p = jnp.exp(s - mn)
        l_scr[...] = l_scr[...] * al + jnp.sum(p, axis=-1, keepdims=True)
        pv = [_dot(p[rows_h * h:rows_h * (h + 1)].astype(BF16), v_of_head(h)) for h in range(N_DA_HEADS)]
        acc_scr[...] = acc_scr[...] * al + jnp.concatenate(pv, axis=0)
        m_scr[...] = mn

    def page_values(h):
        return jnp.concatenate([v_refs[i].at[0, 0][pl.ds(h, PAGE_SIZE, stride=N_DA_HEADS), :].astype(BF16)
                                for i in range(pp)], axis=0)

    s = jnp.concatenate([_dot(q, k_refs[i][0, 0].astype(BF16)) for i in range(pp)], axis=1)
    update(s, page_values)

    @pl.when(p_id == pl.num_programs(1) - 1)
    def _():
        s = _dot_nt(q, kn_ref[0].astype(BF16))
        r = lax.broadcasted_iota(jnp.int32, s.shape, 0)
        t = lax.broadcasted_iota(jnp.int32, s.shape, 1)
        s = jnp.where(t <= (r % n_new), s, MASK_VALUE)
        update(s, lambda h: vn_ref[0, h].astype(BF16))
        n = acc_scr[...] / l_scr[...]
        comb = n - lam_ref[0] * pltpu.roll(n, n.shape[0] - n_new, 0)
        o_ref[0] = _subln(comb, g_ref[...], lam_init)


def _attn_sample(page_table, lam, qrows, k_new, v_new_h, g, cache_kt, cache_v4, layer, lam_init, pp=8):
    Bd, nrow, _ = qrows.shape
    n_new = nrow // (2 * N_DA_HEADS)
    n_pages = page_table.shape[1]
    pt = page_table.reshape(-1)

    def page_spec(i):
        return pl.BlockSpec((1, 1, DA_WIDTH, LANES),
                            lambda b, p, pt_ref: (layer, pt_ref[b * n_pages + p * pp + i], 0, 0))

    per_b = lambda shape: pl.BlockSpec((1,) + shape, lambda b, p, pt_ref: (b,) + (0,) * len(shape))
    grid_spec = pltpu.PrefetchScalarGridSpec(
        num_scalar_prefetch=1,
        grid=(Bd, n_pages // pp),
        in_specs=[pl.BlockSpec(memory_space=pltpu.SMEM), per_b((nrow, DA_WIDTH)), per_b(k_new.shape[1:]),
                  per_b(v_new_h.shape[1:]), pl.BlockSpec((1, DA_V_DIM), lambda b, p, pt_ref: (0, 0))]
                 + [page_spec(i) for i in range(pp)] + [page_spec(i) for i in range(pp)],
        out_specs=per_b((nrow, DA_V_DIM)),
        scratch_shapes=[pltpu.VMEM((nrow, 1), F32), pltpu.VMEM((nrow, 1), F32),
                        pltpu.VMEM((nrow, DA_V_DIM), F32)],
    )
    return pl.pallas_call(
        functools.partial(_dec_attn_kernel, pp=pp, n_new=n_new, lam_init=lam_init),
        grid_spec=grid_spec,
        out_shape=jax.ShapeDtypeStruct((Bd, nrow, DA_V_DIM), F32),
        compiler_params=_cp("parallel", "arbitrary"),
        name="diff_attn_decode",
    )(pt, lam, qrows, k_new, v_new_h, g, *([cache_kt] * pp), *([cache_v4] * pp))


def _rw_prep_kernel(h_ref, sh0_ref, mu_ref, w0_ref, a0_ref, wup_ref, aup_ref, gup_ref,
                    r_ref, k_ref, v_ref, lw_ref, a_ref, g_ref, prev_scr):
    cp = h_ref.shape[1]

    @pl.when(pl.program_id(1) == 0)
    def _():
        prev_scr[...] = sh0_ref[0]

    rw = h_ref[0]
    rowi = lax.broadcasted_iota(jnp.int32, rw.shape, 0)
    prev = jnp.where(rowi == 0, prev_scr[...], pltpu.roll(rw, 1, 0))
    prev_scr[...] = rw[cp - 1:cp, :]
    xm = rw + (prev - rw) * mu_ref[...]
    c1, c2, c3 = RW_WIDTH, 2 * RW_WIDTH, 3 * RW_WIDTH
    r_ref[0] = xm[:, :c1]
    k_ref[0] = xm[:, c1:c2]
    v_ref[0] = xm[:, c2:c3]
    lo = xm[:, c3:c3 + LANES]
    dg = xm[:, c3 + LANES:]
    z = -(w0_ref[...] + _mm3(_split(jnp.tanh(lo)), _split(wup_ref[...])))
    softplus = jnp.maximum(z, 0.0) + jnp.log1p(jnp.exp(-jnp.abs(z)))
    lw_ref[0] = -jnp.exp(-softplus - 0.5)
    a_ref[0] = jax.nn.sigmoid(a0_ref[...] + _mm3(_split(lo), _split(aup_ref[...])))
    g_ref[0] = _mm3(_split(jax.nn.sigmoid(dg)), _split(gup_ref[...]))


def _rw_prep(hrw3, sh0, mu, w0, a0, wup, aup, gup):
    B, T, _ = hrw3.shape
    cp = min(T, 256)
    full = lambda shape: pl.BlockSpec(shape, lambda b, c: (0,) * len(shape))
    out_blk = pl.BlockSpec((1, cp, RW_WIDTH), lambda b, c: (b, c, 0))
    out_sds = jax.ShapeDtypeStruct((B, T, RW_WIDTH), F32)
    return pl.pallas_call(
        _rw_prep_kernel,
        grid=(B, T // cp),
        in_specs=[pl.BlockSpec((1, cp, N_RW_PAD), lambda b, c: (b, c, 0)),
                  pl.BlockSpec((1, 1, N_RW_PAD), lambda b, c: (b, 0, 0)),
                  full((1, N_RW_PAD)), full((1, RW_WIDTH)), full((1, RW_WIDTH)),
                  full((LANES, RW_WIDTH)), full((LANES, RW_WIDTH)), full((G_PAD, RW_WIDTH))],
        out_specs=[out_blk] * 6,
        out_shape=[out_sds] * 6,
        scratch_shapes=[pltpu.VMEM((1, N_RW_PAD), F32)],
        compiler_params=_cp("parallel", "arbitrary"),
        name="rwkv_prep",
    )(hrw3, sh0, mu, w0, a0, wup, aup, gup)


def _split(x):
    hi = x.astype(BF16)
    lo = (x - hi.astype(F32)).astype(BF16)
    return hi, lo


def _mm3(a, b, dot=_dot):
    return dot(a[1], b[0]) + dot(a[0], b[1]) + dot(a[0], b[0])


def _mm_exact_rhs(x, m):
    hi, lo = _split(x)
    return _dot(lo, m) + _dot(hi, m)


def _rw_chunk(args, masks, t_valid):
    C = args[0][0].shape[0]
    incl_b, strict, incl, eye, seg_b, head_a, blockdiag, diag, levels = masks
    head_masks = (head_a, jnp.logical_not(head_a))
    P = range(len(args))
    PH = [(p, hd) for p in P for hd in range(2)]

    pre = []
    for (r, k, v, lw, a, g, kkp, kap, rkp, lnw, lnb, H) in args:
        if t_valid < C:
            valid = lax.broadcasted_iota(jnp.int32, (C, LANES), 0) < t_valid
            lw = jnp.where(valid, lw, 0.0)
            k = jnp.where(valid, k, 0.0)
            v = jnp.where(valid, v, 0.0)
        kk = k * kkp
        kk = kk * lax.rsqrt(jnp.maximum(_mm_exact_rhs(kk * kk, seg_b), 1e-24))
        k2 = k * (1.0 + (a - 1.0) * kap)
        b = kk * a
        lw_hi, lw_lo = _split(lw)
        lw_lo2 = (lw - lw_hi.astype(F32) - lw_lo.astype(F32)).astype(BF16)
        gam = _dot(incl_b, lw_lo2) + _dot(incl_b, lw_lo) + _dot(incl_b, lw_hi)
        gl = gam[C - 1:C, :]
        rt = r * jnp.exp(gam)
        at = -kk * jnp.exp(gam - lw)
        ginv = jnp.exp(-gam)
        gh = jnp.exp(gl - gam)
        pre.append(dict(r=r, v=v, k2=k2, g=g, rkp=rkp, lnw=lnw, lnb=lnb, H=H, gl=gl, rt=rt, at=at,
                        bt=_split(b * ginv), kt=_split(k2 * ginv), bh=b * gh, kh=k2 * gh,
                        Hs=_split(H), vs=_split(v)))

    stacked = C % LANES == 0
    cat0 = lambda xs: jnp.concatenate(xs, axis=0)
    if stacked:
        ar = [cat0([pre[p]['at'], pre[p]['rt']]) for p in P]
        bk = [tuple(cat0([pre[p]['bt'][i], pre[p]['kt'][i]]) for i in range(2)) for p in P]
        head2 = [cat0([hm, hm]) for hm in head_masks]
        A = {(p, hd): _mm3(_split(jnp.where(head2[hd], ar[p], 0.0)), bk[p], _dot_nt) for p, hd in PH}
        aab = {c: jnp.where(strict, A[c][:C, :C], 0.0) for c in PH}
        aak = {c: jnp.where(strict, A[c][:C, C:], 0.0) for c in PH}
        arb = {c: jnp.where(incl, A[c][C:, :C], 0.0) for c in PH}
        ark = {c: jnp.where(incl, A[c][C:, C:], 0.0) for c in PH}
        XR = [_mm3(_split(ar[p]), pre[p]['Hs']) for p in P]
        X = [xr[:C] for xr in XR]
        RH = [xr[C:] for xr in XR]
    else:
        atm = {(p, hd): _split(jnp.where(head_masks[hd], pre[p]['at'], 0.0)) for p, hd in PH}
        rtm = {(p, hd): _split(jnp.where(head_masks[hd], pre[p]['rt'], 0.0)) for p, hd in PH}
        aab = {c: jnp.where(strict, _mm3(atm[c], pre[c[0]]['bt'], _dot_nt), 0.0) for c in PH}
        aak = {c: jnp.where(strict, _mm3(atm[c], pre[c[0]]['kt'], _dot_nt), 0.0) for c in PH}
        arb = {c: jnp.where(incl, _mm3(rtm[c], pre[c[0]]['bt'], _dot_nt), 0.0) for c in PH}
        ark = {c: jnp.where(incl, _mm3(rtm[c], pre[c[0]]['kt'], _dot_nt), 0.0) for c in PH}
        X = [_mm3(_split(pre[p]['at']), pre[p]['Hs']) for p in P]
        RH = [_mm3(_split(pre[p]['rt']), pre[p]['Hs']) for p in P]
    tinv = {c: eye + jnp.where(levels[0], aab[c], 0.0) for c in PH}
    for lvl in levels[1:]:
        ts = {c: _split(tinv[c]) for c in PH}
        mid = {c: _split(_mm3(_split(jnp.where(lvl, aab[c], 0.0)), ts[c])) for c in PH}
        tinv = {c: tinv[c] + _mm3(ts[c], mid[c]) for c in PH}
    if stacked:
        ys2 = [_mm3(_split(cat0([aak[(p, 0)], aak[(p, 1)]])), pre[p]['vs']) for p in P]
        ysum = [_split(X[p] + jnp.where(head_a, ys2[p][:C], ys2[p][C:])) for p in P]
        u2 = [_mm3(_split(cat0([tinv[(p, 0)], tinv[(p, 1)]])), ysum[p]) for p in P]
        U = [jnp.where(head_a, u2[p][:C], u2[p][C:]) for p in P]
    else:
        ys = {c: _mm3(_split(aak[c]), pre[c[0]]['vs']) for c in PH}
        ysum = [_split(X[p] + jnp.where(head_a, ys[(p, 0)], ys[(p, 1)])) for p in P]
        u = {c: _mm3(_split(tinv[c]), ysum[c[0]]) for c in PH}
        U = [jnp.where(head_a, u[(p, 0)], u[(p, 1)]) for p in P]

    hn, uvh = [], []
    for p in P:
        dg = jnp.where(diag, jnp.exp(pre[p]['gl']), 0.0)
        lhs = _split(cat0([pre[p]['bh'], pre[p]['kh'], dg]))
        uvh.append(_split(cat0([U[p], pre[p]['v'], pre[p]['H']])))
        hn.append(jnp.where(blockdiag, _mm3(lhs, uvh[p], _dot_tn), 0.0))

    if stacked:
        both = lambda p, hd: jnp.concatenate([arb[(p, hd)], ark[(p, hd)]], axis=1)
        dd2 = [_mm3(_split(cat0([both(p, 0), both(p, 1)])), tuple(t[:2 * C] for t in uvh[p])) for p in P]
        dsel = [jnp.where(head_a, dd2[p][:C], dd2[p][C:]) for p in P]
    else:
        Us = [_split(U[p]) for p in P]
        dd = {c: _mm3(_split(arb[c]), Us[c[0]]) + _mm3(_split(ark[c]), pre[c[0]]['vs']) for c in PH}
        dsel = [jnp.where(head_a, dd[(p, 0)], dd[(p, 1)]) for p in P]
    inv_n = 1.0 / RW_HEAD
    outs = []
    for p in P:
        q = pre[p]
        O = RH[p] + dsel[p]
        mean = _mm_exact_rhs(O, seg_b) * inv_n
        d = O - mean
        var = _mm_exact_rhs(d * d, seg_b) * inv_n
        on = d * lax.rsqrt(var + LNX_EPS) * q['lnw'] + q['lnb']
        bonus = _mm_exact_rhs(q['r'] * q['k2'] * q['rkp'], seg_b) * q['v']
        outs.append(((on + bonus) * q['g'], hn[p]))
    return outs


def _rw_scan_kernel(r_ref, k_ref, v_ref, lw_ref, a_ref, g_ref, kk_ref, ka_ref, rk_ref,
                    lnw_ref, lnb_ref, h0_ref, o_ref, hout_ref, h_scr, *, t_valid):
    C = r_ref.shape[1]
    n_pairs = r_ref.shape[2] // LANES
    ci = pl.program_id(2)

    @pl.when(ci == 0)
    def _():
        h_scr[...] = h0_ref[0]

    ti = lax.broadcasted_iota(jnp.int32, (C, C), 0)
    si = lax.broadcasted_iota(jnp.int32, (C, C), 1)
    hr = lax.broadcasted_iota(jnp.int32, (LANES, LANES), 0)
    hc = lax.broadcasted_iota(jnp.int32, (LANES, LANES), 1)
    blockdiag = (hr // RW_HEAD) == (hc // RW_HEAD)
    levels = []
    m = 1
    while m < C:
        levels.append(((ti // (2 * m)) == (si // (2 * m))) & ((ti % (2 * m)) >= m) & ((si % (2 * m)) < m))
        m *= 2
    masks = ((si <= ti).astype(BF16), si < ti, si <= ti, (si == ti).astype(F32),
             blockdiag.astype(BF16),
             lax.broadcasted_iota(jnp.int32, (C, LANES), 1) < RW_HEAD, blockdiag, hr == hc, levels)
    sls = [slice(LANES * pi, LANES * (pi + 1)) for pi in range(n_pairs)]
    args = [(r_ref[0, :, sl], k_ref[0, :, sl], v_ref[0, :, sl], lw_ref[0, :, sl], a_ref[0, :, sl],
             g_ref[0, :, sl], kk_ref[:, sl], ka_ref[:, sl], rk_ref[:, sl], lnw_ref[:, sl], lnb_ref[:, sl],
             h_scr[pi]) for pi, sl in enumerate(sls)]
    res = _rw_chunk(args, masks, t_valid)
    for pi, sl in enumerate(sls):
        o_ref[0, :, sl] = res[pi][0].astype(o_ref.dtype)
        h_scr[pi] = res[pi][1]

    @pl.when(ci == pl.num_programs(2) - 1)
    def _():
        hout_ref[0] = h_scr[...]


def _rw_scan(r, k, v, lw, a, g, kk, ka, rk, lnw, lnb, h0, t_valid, out_dtype):
    B, T, _ = r.shape
    C = min(T, 128)
    pb = 4 if C == LANES else N_RW_PAIRS
    w = LANES * pb
    tok = pl.BlockSpec((1, C, w), lambda b, p, c: (b, c, p))
    vec = pl.BlockSpec((1, w), lambda b, p, c: (0, p))
    st = pl.BlockSpec((1, pb, LANES, LANES), lambda b, p, c: (b, p, 0, 0))
    return pl.pallas_call(
        functools.partial(_rw_scan_kernel, t_valid=t_valid),
        grid=(B, N_RW_PAIRS // pb, T // C),
        in_specs=[tok] * 6 + [vec] * 5 + [st],
        out_specs=[tok, st],
        out_shape=[jax.ShapeDtypeStruct((B, T, RW_WIDTH), out_dtype),
                   jax.ShapeDtypeStruct((B, N_RW_PAIRS, LANES, LANES), F32)],
        scratch_shapes=[pltpu.VMEM((pb, LANES, LANES), F32)],
        compiler_params=_cp("parallel", "parallel", "arbitrary"),
        name="rwkv_scan",
    )(r, k, v, lw, a, g, kk, ka, rk, lnw, lnb, h0)


def _out_proj_kernel(a1_ref, a2_ref, w_ref, res_ref, o_ref):
    acc = _dot(a1_ref[...].astype(BF16), w_ref[:DA_WIDTH, :])
    acc = acc + _dot(a2_ref[...].astype(BF16), w_ref[DA_WIDTH:, :])
    o_ref[...] = res_ref[...] + acc


def _out_proj(a1, a2, w, res):
    M, D = res.shape
    tm = min(M, 1024)
    tn = 512
    return pl.pallas_call(
        _out_proj_kernel,
        grid=(M // tm, D // tn),
        in_specs=[pl.BlockSpec((tm, DA_WIDTH), lambda i, j: (i, 0)),
                  pl.BlockSpec((tm, RW_WIDTH), lambda i, j: (i, 0)),
                  pl.BlockSpec((DA_WIDTH + RW_WIDTH, tn), lambda i, j: (0, j)),
                  pl.BlockSpec((tm, tn), lambda i, j: (i, j))],
        out_specs=pl.BlockSpec((tm, tn), lambda i, j: (i, j)),
        out_shape=jax.ShapeDtypeStruct((M, D), F32),
        compiler_params=_cp("parallel", "parallel"),
        name="out_proj",
    )(a1, a2, w, res)


def _ffn_up_kernel(x_ref, g_ref, wg_ref, wu_ref, o_ref, xn_ref):
    @pl.when(pl.program_id(1) == 0)
    def _():
        x = x_ref[...]
        ms = jnp.mean(x * x, axis=-1, keepdims=True)
        xn_ref[...] = (x * lax.rsqrt(ms + NORM_EPS) * g_ref[...]).astype(BF16)

    xn = xn_ref[...]
    gate = _dot(xn, wg_ref[...])
    up = _dot(xn, wu_ref[...])
    o_ref[...] = (gate * jax.nn.sigmoid(gate) * up).astype(o_ref.dtype)


def _ffn_up(x, g, wg, wu):
    M, D = x.shape
    tm = min(M, 1024)
    tn = 512
    return pl.pallas_call(
        _ffn_up_kernel,
        grid=(M // tm, D_FF // tn),
        in_specs=[pl.BlockSpec((tm, D), lambda i, j: (i, 0)),
                  pl.BlockSpec((1, D), lambda i, j: (0, 0)),
                  pl.BlockSpec((D, tn), lambda i, j: (0, j)),
                  pl.BlockSpec((D, tn), lambda i, j: (0, j))],
        out_specs=pl.BlockSpec((tm, tn), lambda i, j: (i, j)),
        out_shape=jax.ShapeDtypeStruct((M, D_FF), BF16),
        scratch_shapes=[pltpu.VMEM((tm, D), BF16)],
        compiler_params=_cp("parallel", "arbitrary"),
        name="ffn_gate_up",
    )(x, g, wg, wu)


def _ffn_down_kernel(a_ref, w_ref, res_ref, o_ref):
    o_ref[...] = res_ref[...] + _dot(a_ref[...], w_ref[...])


def _ffn_down(a, w, res):
    M, D = res.shape
    tm = min(M, 1024)
    tn = 512
    return pl.pallas_call(
        _ffn_down_kernel,
        grid=(M // tm, D // tn),
        in_specs=[pl.BlockSpec((tm, D_FF), lambda i, j: (i, 0)),
                  pl.BlockSpec((D_FF, tn), lambda i, j: (0, j)),
                  pl.BlockSpec((tm, tn), lambda i, j: (i, j))],
        out_specs=pl.BlockSpec((tm, tn), lambda i, j: (i, j)),
        out_shape=jax.ShapeDtypeStruct((M, D), F32),
        compiler_params=_cp("parallel", "parallel"),
        name="ffn_down",
    )(a, w, res)


def _rope_tables(pos):
    half = DA_QK_DIM // 2
    inv_freq = jnp.power(ROPE_THETA, -jnp.arange(half, dtype=F32) / half)
    ang = pos.astype(F32)[:, None] * inv_freq[None, :]
    cos = jnp.cos(ang)
    sin = jnp.sin(ang)
    reps = LANES // DA_QK_DIM
    cos_t = jnp.tile(jnp.concatenate([cos, cos], axis=-1), (1, reps))
    sin_t = jnp.tile(jnp.concatenate([-sin, sin], axis=-1), (1, reps))
    return cos_t, sin_t


def _state_to_pairs(S):
    B = S.shape[0]
    St = jnp.swapaxes(S, -1, -2).reshape(B, N_RW_PAIRS, 2, RW_HEAD, RW_HEAD)
    eye2 = jnp.eye(2, dtype=S.dtype)
    Hp = St[:, :, :, :, None, :] * eye2[None, None, :, None, :, None]
    return Hp.reshape(B, N_RW_PAIRS, LANES, LANES)


def _pairs_to_state(Hp):
    B = Hp.shape[0]
    H5 = Hp.reshape(B, N_RW_PAIRS, 2, RW_HEAD, 2, RW_HEAD)
    diag = jnp.stack([H5[:, :, 0, :, 0, :], H5[:, :, 1, :, 1, :]], axis=2)
    return jnp.swapaxes(diag.reshape(B, N_RW_HEADS, RW_HEAD, RW_HEAD), -1, -2)


def _decode_query_rows(qb):
    Bd = qb.shape[0]
    sub = jnp.arange(DA_WIDTH) // DA_QK_DIM
    want = 2 * jnp.arange(N_DA_HEADS)[:, None] + jnp.arange(2)[None, :]
    mask = sub[None, None, :] == want[:, :, None]
    rows = jnp.where(mask[None, :, :, None, :], qb[:, None, None, :, :], jnp.zeros((), qb.dtype))
    return rows.reshape(Bd, -1, DA_WIDTH)


def _decode_head_rows(o, n_new):
    Bd = o.shape[0]
    o5 = o.reshape(Bd, N_DA_HEADS, 2, n_new, DA_V_DIM)[:, :, 0]
    return jnp.transpose(o5, (0, 2, 1, 3)).reshape(Bd, n_new, DA_WIDTH)


def _pad_rows(x, n):
    return jnp.pad(x, ((0, 0), (0, n - x.shape[1]), (0, 0)))


def _layer(x3, cos, sin, S0, shift0, lam, lam_init, p, decode=None):
    B, T, D = x3.shape
    M = B * T
    x = x3.reshape(M, D)
    hda = _norm_mm(x, p['norm_mix'], p['w_da'], 1024)
    hrw = _norm_mm(x, p['norm_mix'], p['w_rw'], 1152)
    qb, kf, kb, vb = _qk_post(hda, p['q_gain'], p['k_gain'], cos, sin, T)
    hda3 = hda.reshape(B, T, 3 * DA_WIDTH)
    v_new = hda3[:, :, 2 * DA_WIDTH:]
    if decode is None:
        o_da = _attn_prompt(lam, qb.reshape(B, T, DA_WIDTH), kb.reshape(B, T, DA_WIDTH),
                            vb.reshape(B, T, DA_WIDTH), p['subln_gain'], lam_init)
    else:
        page_table, cache_kt, cache_v4, layer = decode
        v_new_h = jnp.transpose(v_new.reshape(B, T, N_DA_HEADS, DA_V_DIM), (0, 2, 1, 3))
        v_new_h = jnp.pad(v_new_h, ((0, 0), (0, 0), (0, NEW_PAD - T), (0, 0)))
        o_da = _attn_sample(page_table, lam, _decode_query_rows(qb.reshape(B, T, DA_WIDTH)),
                            _pad_rows(kf.reshape(B, T, DA_WIDTH), NEW_PAD), v_new_h, p['subln_gain'],
                            cache_kt, cache_v4, layer, lam_init)
        o_da = _decode_head_rows(o_da, T)
    hrw3 = hrw.reshape(B, T, N_RW_PAD)
    shift_new = hrw3[:, -1, :N_RW_COLS]
    t_pad = max(T, 8)
    hrw3p = _pad_rows(hrw3, t_pad)
    sh0 = jnp.pad(shift0, ((0, 0), (0, N_RW_PAD - N_RW_COLS)))[:, None, :]
    r, k, v, lw, a, g = _rw_prep(hrw3p, sh0, p['mu_shift'], p['w0'], p['a0'], p['w_up'], p['a_up'],
                                 p['g_up'])
    o_rw, hp = _rw_scan(r, k, v, lw, a, g, p['k_k'], p['k_a'], p['r_k'], p['lnx_w'], p['lnx_b'],
                        _state_to_pairs(S0), T, BF16 if decode is None else F32)
    o_rw = o_rw[:, :T]
    S_new = _pairs_to_state(hp)
    x1 = _out_proj(o_da.reshape(M, DA_WIDTH), o_rw.reshape(M, RW_WIDTH), p['w_out'], x)
    ff = _ffn_up(x1, p['norm_ffn'], p['w_gate'], p['w_up_ffn'])
    x2 = _ffn_down(ff, p['w_down'], x1)
    return x2.reshape(B, T, D), kf, v_new, S_new, shift_new


def kernel(x_prompt, x_sample, cache_k, cache_v, state_wkv, state_shift, page_table, norm_mix, w_in, q_gain, k_gain, lambda_q1, lambda_k1, lambda_q2, lambda_k2, subln_gain, mu_shift, w0, w_up, a0, a_up, g_up, k_k, k_a, r_k, lnx_w, lnx_b, w_out, norm_ffn, w_gate, w_up_ffn, w_down):
    B, S, _ = x_prompt.shape
    Bd, T, _ = x_sample.shape
    depth = w_in.shape[0]
    n_pages = page_table.shape[1]
    past_len = n_pages * PAGE_SIZE
    cos_p, sin_p = _rope_tables(jnp.arange(S, dtype=jnp.int32))
    cos_s, sin_s = _rope_tables(past_len + jnp.arange(T, dtype=jnp.int32))
    cos_s = jnp.tile(cos_s, (Bd, 1))
    sin_s = jnp.tile(sin_s, (Bd, 1))
    n_pool = cache_k.shape[1]
    cache_kt = jnp.transpose(cache_k, (0, 1, 3, 4, 2)).reshape(depth, n_pool, DA_WIDTH, PAGE_SIZE)
    cache_v4 = cache_v.reshape(depth, n_pool, PAGE_SIZE * N_DA_HEADS, DA_V_DIM)
    row = lambda t: t.reshape(1, -1)
    tile_gain = lambda t: jnp.tile(t, LANES // DA_QK_DIM).reshape(1, LANES)

    xp, xs = x_prompt, x_sample
    outs = [[] for _ in range(8)]
    for l in range(depth):
        lam_init = 0.8 - 0.6 * math.exp(-0.3 * l)
        lam = (jnp.exp(jnp.sum(lambda_q1[l] * lambda_k1[l])) - jnp.exp(jnp.sum(lambda_q2[l] * lambda_k2[l]))
               + lam_init).reshape(1).astype(F32)
        w_in_b = w_in[l].astype(BF16)
        p = {
            'norm_mix': row(norm_mix[l]),
            'w_da': w_in_b[:, :3 * DA_WIDTH],
            'w_rw': jnp.pad(w_in_b[:, 3 * DA_WIDTH:], ((0, 0), (0, N_RW_PAD - N_RW_COLS))),
            'q_gain': tile_gain(q_gain[l]), 'k_gain': tile_gain(k_gain[l]),
            'subln_gain': row(subln_gain[l]),
            'mu_shift': jnp.pad(row(mu_shift[l]), ((0, 0), (0, N_RW_PAD - N_RW_COLS))),
            'w0': row(w0[l]), 'a0': row(a0[l]),
            'w_up': jnp.pad(w_up[l], ((0, LANES - W_LORA), (0, 0))),
            'a_up': jnp.pad(a_up[l], ((W_LORA, LANES - W_LORA - A_LORA), (0, 0))),
            'g_up': jnp.pad(g_up[l], ((0, G_PAD - G_LORA), (0, 0))),
            'k_k': row(k_k[l]), 'k_a': row(k_a[l]), 'r_k': row(r_k[l]),
            'lnx_w': row(lnx_w[l]), 'lnx_b': row(lnx_b[l]),
            'w_out': w_out[l].astype(BF16), 'norm_ffn': row(norm_ffn[l]),
            'w_gate': w_gate[l].astype(BF16), 'w_up_ffn': w_up_ffn[l].astype(BF16),
            'w_down': w_down[l].astype(BF16),
        }
        S0_p = jnp.zeros((B, N_RW_HEADS, RW_HEAD, RW_HEAD), F32)
        sh0_p = jnp.zeros((B, N_RW_COLS), F32)
        xp, k_new, v_new, S_new, sh_new = _layer(xp, cos_p, sin_p, S0_p, sh0_p, lam, lam_init, p)
        outs[0].append(k_new.reshape(B, S // PAGE_SIZE, PAGE_SIZE, N_DA_SUB, DA_QK_DIM))
        outs[1].append(v_new.reshape(B, S // PAGE_SIZE, PAGE_SIZE, N_DA_HEADS, DA_V_DIM))
        outs[2].append(S_new)
        outs[3].append(sh_new)
        xs, k_new, v_new, S_new, sh_new = _layer(xs, cos_s, sin_s, state_wkv[l], state_shift[l], lam,
                                                 lam_init, p, decode=(page_table, cache_kt, cache_v4, l))
        outs[4].append(k_new.reshape(Bd, T, N_DA_SUB, DA_QK_DIM))
        outs[5].append(v_new.reshape(Bd, T, N_DA_HEADS, DA_V_DIM))
        outs[6].append(S_new)
        outs[7].append(sh_new)
    return (xp, xs) + tuple(jnp.stack(o) for o in outs)
```

```python
import functools
import math

import jax
import jax.numpy as jnp
from jax import lax
from jax.experimental import pallas as pl
from jax.experimental.pallas import tpu as pltpu

F32 = jnp.float32
BF16 = jnp.bfloat16

D_MODEL = 2048
PAGE_SIZE = 128
DA_WIDTH = 1024
RW_WIDTH = 1024
DA_V_DIM = 128
N_DA_HEADS = 8
DA_QK_DIM = 64
N_DA_SUB = 16
RW_HEAD = 64
N_RW_HEADS = 16
N_RW_PAIRS = N_RW_HEADS // 2
W_LORA = 64
A_LORA = 64
G_LORA = 160
N_RW_COLS = 3 * RW_WIDTH + W_LORA + A_LORA + G_LORA
LANES = 128
N_RW_PAD = ((N_RW_COLS + LANES - 1) // LANES) * LANES
G_PAD = N_RW_PAD - 3 * RW_WIDTH - LANES
D_FF = 5632
ROPE_THETA = 10000.0
NORM_EPS = 1e-6
LNX_EPS = 64e-5
MASK_VALUE = -1e30
DA_SCALE = DA_QK_DIM ** -0.5
VMEM_LIMIT_BYTES = 56 * 1024 * 1024
NEW_PAD = 16


def _cp(*sem):
    return pltpu.CompilerParams(dimension_semantics=sem, vmem_limit_bytes=VMEM_LIMIT_BYTES)


def _dot(a, b, precision=None):
    return jnp.dot(a, b, preferred_element_type=F32, precision=precision)


def _dot_nt(a, b, precision=None):
    return lax.dot_general(a, b, (((1,), (1,)), ((), ())), preferred_element_type=F32,
                           precision=precision)


def _dot_tn(a, b, precision=None):
    return lax.dot_general(a, b, (((0,), (0,)), ((), ())), preferred_element_type=F32,
                           precision=precision)


def _seg_ones(n, seg):
    r = lax.broadcasted_iota(jnp.int32, (n, n), 0) // seg
    c = lax.broadcasted_iota(jnp.int32, (n, n), 1) // seg
    return (r == c).astype(F32)


def _norm_mm_kernel(x_ref, g_ref, w_ref, o_ref, xn_ref):
    @pl.when(pl.program_id(1) == 0)
    def _():
        x = x_ref[...]
        ms = jnp.mean(x * x, axis=-1, keepdims=True)
        xn_ref[...] = (x * lax.rsqrt(ms + NORM_EPS) * g_ref[...]).astype(BF16)

    o_ref[...] = _dot(xn_ref[...], w_ref[...])


def _norm_mm(x, g, w, tn):
    M, D = x.shape
    N = w.shape[1]
    tm = min(M, 1024)
    return pl.pallas_call(
        _norm_mm_kernel,
        grid=(M // tm, N // tn),
        in_specs=[pl.BlockSpec((tm, D), lambda i, j: (i, 0)),
                  pl.BlockSpec((1, D), lambda i, j: (0, 0)),
                  pl.BlockSpec((D, tn), lambda i, j: (0, j))],
        out_specs=pl.BlockSpec((tm, tn), lambda i, j: (i, j)),
        out_shape=jax.ShapeDtypeStruct((M, N), F32),
        scratch_shapes=[pltpu.VMEM((tm, D), BF16)],
        compiler_params=_cp("parallel", "arbitrary"),
        name="norm_in_proj",
    )(x, g, w)


def _qk_post_kernel(hq_ref, hk_ref, hv_ref, qg_ref, kg_ref, cos_ref, sin_ref, qb_ref, kf_ref, kb_ref, vb_ref):
    tm = hq_ref.shape[0]
    vb_ref[...] = hv_ref[...].astype(BF16)
    lane = lax.broadcasted_iota(jnp.int32, (tm, LANES), 1)
    first = (lane % DA_QK_DIM) < (DA_QK_DIM // 2)
    seg = (_seg_ones(LANES, DA_QK_DIM) * (1.0 / DA_QK_DIM)).astype(BF16)
    cos = cos_ref[...]
    sin = sin_ref[...]

    def proc(x, g):
        ms = _mm_exact_rhs(x * x, seg)
        y = x * lax.rsqrt(ms + NORM_EPS) * g
        partner = jnp.where(first, pltpu.roll(y, LANES - DA_QK_DIM // 2, 1),
                            pltpu.roll(y, DA_QK_DIM // 2, 1))
        return y * cos + partner * sin

    for c in range(DA_WIDTH // LANES):
        sl = slice(LANES * c, LANES * (c + 1))
        q = proc(hq_ref[:, sl], qg_ref[...])
        qb_ref[:, sl] = (q * DA_SCALE).astype(BF16)
        k = proc(hk_ref[:, sl], kg_ref[...])
        kf_ref[:, sl] = k
        kb_ref[:, sl] = k.astype(BF16)


def _qk_post(hda, qg, kg, cos, sin, T):
    M = hda.shape[0]
    tm = min(M, 512)
    if cos.shape[0] == M:
        tab_map = lambda i: (i, 0)
    else:
        nper = T // tm
        tab_map = lambda i: (i % nper, 0)
    blk = lambda c: pl.BlockSpec((tm, DA_WIDTH), lambda i: (i, c))
    vec = pl.BlockSpec((1, LANES), lambda i: (0, 0))
    tab = pl.BlockSpec((tm, LANES), tab_map)
    return pl.pallas_call(
        _qk_post_kernel,
        grid=(M // tm,),
        in_specs=[blk(0), blk(1), blk(2), vec, vec, tab, tab],
        out_specs=[blk(0), blk(0), blk(0), blk(0)],
        out_shape=[jax.ShapeDtypeStruct((M, DA_WIDTH), BF16),
                   jax.ShapeDtypeStruct((M, DA_WIDTH), F32),
                   jax.ShapeDtypeStruct((M, DA_WIDTH), BF16),
                   jax.ShapeDtypeStruct((M, DA_WIDTH), BF16)],
        compiler_params=_cp("parallel"),
        name="qk_norm_rope",
    )(hda, hda, hda, qg, kg, cos, sin)


def _subln(o, g, lam_init):
    ms = jnp.mean(o * o, axis=-1, keepdims=True)
    return o * lax.rsqrt(ms + NORM_EPS) * g * (1.0 - lam_init)


def _attn_kernel(lam_ref, q_ref, k_ref, v_ref, g_ref, o_ref, m_scr, l_scr, acc_scr, *, tq, tk, lam_init):
    qi = pl.program_id(2)
    n_heads = q_ref.shape[2] // LANES
    lane = lax.broadcasted_iota(jnp.int32, (tq, LANES), 1)
    chains = [(hd, mi) for hd in range(n_heads) for mi in range(2)]
    qs = []
    for hd, mi in chains:
        q = q_ref[0, :, LANES * hd:LANES * (hd + 1)]
        keep = (lane < DA_QK_DIM) if mi == 0 else (lane >= DA_QK_DIM)
        qs.append(jnp.where(keep, q, jnp.zeros_like(q)))
    m_scr[...] = jnp.full(m_scr.shape, MASK_VALUE, F32)
    l_scr[...] = jnp.zeros(l_scr.shape, F32)
    acc_scr[...] = jnp.zeros(acc_scr.shape, F32)

    def block(j, masked):
        off = pl.multiple_of(j * tk, tk)
        kbs = [k_ref[0, pl.ds(off, tk), LANES * hd:LANES * (hd + 1)] for hd in range(n_heads)]
        vbs = [v_ref[0, pl.ds(off, tk), LANES * hd:LANES * (hd + 1)] for hd in range(n_heads)]
        nc = range(len(chains))
        ms = [m_scr[c] for c in nc]
        ls = [l_scr[c] for c in nc]
        accs = [acc_scr[c] for c in nc]
        ss = [_dot_nt(qs[c], kbs[chains[c][0]]) for c in nc]
        if masked:
            row = qi * tq + lax.broadcasted_iota(jnp.int32, (tq, tk), 0)
            col = off + lax.broadcasted_iota(jnp.int32, (tq, tk), 1)
            ss = [jnp.where(col <= row, s, MASK_VALUE) for s in ss]
        mns = [jnp.maximum(ms[c], jnp.max(ss[c], axis=-1, keepdims=True)) for c in nc]
        als = [jnp.exp(ms[c] - mns[c]) for c in nc]
        ps = [jnp.exp(ss[c] - mns[c]) for c in nc]
        sums = [functools.reduce(lambda a, b: a + b, [ps[c][:, LANES * i:LANES * (i + 1)]
                                                      for i in range(tk // LANES)]) for c in nc]
        pvs = [_dot(ps[c].astype(BF16), vbs[chains[c][0]]) for c in nc]
        for c in nc:
            m_scr[c] = mns[c]
            l_scr[c] = ls[c] * als[c] + sums[c]
            acc_scr[c] = accs[c] * als[c] + pvs[c]

    n_full = (qi * tq) // tk

    def body(j, c):
        block(j, False)
        return c

    lax.fori_loop(0, n_full, body, 0)
    block(n_full, True)
    for hd in range(n_heads):
        c1, c2 = 2 * hd, 2 * hd + 1
        l1 = jnp.sum(l_scr[c1], axis=-1, keepdims=True)
        l2 = jnp.sum(l_scr[c2], axis=-1, keepdims=True)
        o = acc_scr[c1] / l1 - lam_ref[0] * (acc_scr[c2] / l2)
        o_ref[0, :, LANES * hd:LANES * (hd + 1)] = _subln(o, g_ref[...], lam_init).astype(o_ref.dtype)


def _attn_prompt(lam, qb, kb, vb, g, lam_init):
    B, S, _ = qb.shape
    tq = min(S, 128)
    tk = min(S, 256)
    hb = 2
    w = LANES * hb
    return pl.pallas_call(
        functools.partial(_attn_kernel, tq=tq, tk=tk, lam_init=lam_init),
        grid=(B, N_DA_HEADS // hb, S // tq),
        in_specs=[pl.BlockSpec(memory_space=pltpu.SMEM),
                  pl.BlockSpec((1, tq, w), lambda b, h, i: (b, i, h)),
                  pl.BlockSpec((1, S, w), lambda b, h, i: (b, 0, h)),
                  pl.BlockSpec((1, S, w), lambda b, h, i: (b, 0, h)),
                  pl.BlockSpec((1, DA_V_DIM), lambda b, h, i: (0, 0))],
        out_specs=pl.BlockSpec((1, tq, w), lambda b, h, i: (b, i, h)),
        out_shape=jax.ShapeDtypeStruct((B, S, DA_WIDTH), BF16),
        scratch_shapes=[pltpu.VMEM((2 * hb, tq, 1), F32), pltpu.VMEM((2 * hb, tq, LANES), F32),
                        pltpu.VMEM((2 * hb, tq, DA_V_DIM), F32)],
        compiler_params=_cp("parallel", "parallel", "parallel"),
        name="diff_attn_prompt",
    )(lam, qb, kb, vb, g)


def _dec_attn_kernel(pt_ref, lam_ref, q_ref, kn_ref, vn_ref, g_ref, *rest, pp, n_new, lam_init):
    k_refs = rest[:pp]
    v_refs = rest[pp:2 * pp]
    o_ref = rest[2 * pp]
    m_scr, l_scr, acc_scr = rest[2 * pp + 1:]
    p_id = pl.program_id(1)
    rows_h = 2 * n_new

    @pl.when(p_id == 0)
    def _():
        m_scr[...] = jnp.full(m_scr.shape, MASK_VALUE, F32)
        l_scr[...] = jnp.zeros(l_scr.shape, F32)
        acc_scr[...] = jnp.zeros(acc_scr.shape, F32)

    q = q_ref[0]

    def update(s, v_of_head):
        m = m_scr[...]
        mn = jnp.maximum(m, jnp.max(s, axis=-1, keepdims=True))
        al = jnp.exp(m - mn)
        p = jnp.exp(s - mn)
        l_scr[...] = l_scr[...] * al + jnp.sum(p, axis=-1, keepdims=True)
        pv = [_dot(p[rows_h * h:rows_h * (h + 1)].astype(BF16), v_of_head(h)) for h in range(N_DA_HEADS)]
        acc_scr[...] = acc_scr[...] * al + jnp.concatenate(pv, axis=0)
        m_scr[...] = mn

    def page_values(h):
        return jnp.concatenate([v_refs[i].at[0, 0][pl.ds(h, PAGE_SIZE, stride=N_DA_HEADS), :].astype(BF16)
                                for i in range(pp)], axis=0)

    s = jnp.concatenate([_dot(q, k_refs[i][0, 0].astype(BF16)) for i in range(pp)], axis=1)
    update(s, page_values)

    @pl.when(p_id == pl.num_programs(1) - 1)
    def _():
        s = _dot_nt(q, kn_ref[0].astype(BF16))
        r = lax.broadcasted_iota(jnp.int32, s.shape, 0)
        t = lax.broadcasted_iota(jnp.int32, s.shape, 1)
        s = jnp.where(t <= (r % n_new), s, MASK_VALUE)
        update(s, lambda h: vn_ref[0, h].astype(BF16))
        n = acc_scr[...] / l_scr[...]
        comb = n - lam_ref[0] * pltpu.roll(n, n.shape[0] - n_new, 0)
        o_ref[0] = _subln(comb, g_ref[...], lam_init)


def _attn_sample(page_table, lam, qrows, k_new, v_new_h, g, cache_kt, cache_v4, layer, lam_init, pp=8):
    Bd, nrow, _ = qrows.shape
    n_new = nrow // (2 * N_DA_HEADS)
    n_pages = page_table.shape[1]
    pt = page_table.reshape(-1)

    def page_spec(i):
        return pl.BlockSpec((1, 1, DA_WIDTH, LANES),
                            lambda b, p, pt_ref: (layer, pt_ref[b * n_pages + p * pp + i], 0, 0))

    per_b = lambda shape: pl.BlockSpec((1,) + shape, lambda b, p, pt_ref: (b,) + (0,) * len(shape))
    grid_spec = pltpu.PrefetchScalarGridSpec(
        num_scalar_prefetch=1,
        grid=(Bd, n_pages // pp),
        in_specs=[pl.BlockSpec(memory_space=pltpu.SMEM), per_b((nrow, DA_WIDTH)), per_b(k_new.shape[1:]),
                  per_b(v_new_h.shape[1:]), pl.BlockSpec((1, DA_V_DIM), lambda b, p, pt_ref: (0, 0))]
                 + [page_spec(i) for i in range(pp)] + [page_spec(i) for i in range(pp)],
        out_specs=per_b((nrow, DA_V_DIM)),
        scratch_shapes=[pltpu.VMEM((nrow, 1), F32), pltpu.VMEM((nrow, 1), F32),
                        pltpu.VMEM((nrow, DA_V_DIM), F32)],
    )
    return pl.pallas_call(
        functools.partial(_dec_attn_kernel, pp=pp, n_new=n_new, lam_init=lam_init),
        grid_spec=grid_spec,
        out_shape=jax.ShapeDtypeStruct((Bd, nrow, DA_V_DIM), F32),
        compiler_params=_cp("parallel", "arbitrary"),
        name="diff_attn_decode",
    )(pt, lam, qrows, k_new, v_new_h, g, *([cache_kt] * pp), *([cache_v4] * pp))


def _rw_prep_kernel(h_ref, sh0_ref, mu_ref, w0_ref, a0_ref, wup_ref, aup_ref, gup_ref,
                    r_ref, k_ref, v_ref, lw_ref, a_ref, g_ref, prev_scr):
    cp = h_ref.shape[1]

    @pl.when(pl.program_id(1) == 0)
    def _():
        prev_scr[...] = sh0_ref[0]

    rw = h_ref[0]
    rowi = lax.broadcasted_iota(jnp.int32, rw.shape, 0)
    prev = jnp.where(rowi == 0, prev_scr[...], pltpu.roll(rw, 1, 0))
    prev_scr[...] = rw[cp - 1:cp, :]
    xm = rw + (prev - rw) * mu_ref[...]
    c1, c2, c3 = RW_WIDTH, 2 * RW_WIDTH, 3 * RW_WIDTH
    r_ref[0] = xm[:, :c1]
    k_ref[0] = xm[:, c1:c2]
    v_ref[0] = xm[:, c2:c3]
    lo = xm[:, c3:c3 + LANES]
    dg = xm[:, c3 + LANES:]
    z = -(w0_ref[...] + _mm3(_split(jnp.tanh(lo)), _split(wup_ref[...])))
    softplus = jnp.maximum(z, 0.0) + jnp.log1p(jnp.exp(-jnp.abs(z)))
    lw_ref[0] = -jnp.exp(-softplus - 0.5)
    a_ref[0] = jax.nn.sigmoid(a0_ref[...] + _mm3(_split(lo), _split(aup_ref[...])))
    g_ref[0] = _mm3(_split(jax.nn.sigmoid(dg)), _split(gup_ref[...]))


def _rw_prep(hrw3, sh0, mu, w0, a0, wup, aup, gup):
    B, T, _ = hrw3.shape
    cp = min(T, 256)
    full = lambda shape: pl.BlockSpec(shape, lambda b, c: (0,) * len(shape))
    out_blk = pl.BlockSpec((1, cp, RW_WIDTH), lambda b, c: (b, c, 0))
    out_sds = jax.ShapeDtypeStruct((B, T, RW_WIDTH), F32)
    return pl.pallas_call(
        _rw_prep_kernel,
        grid=(B, T // cp),
        in_specs=[pl.BlockSpec((1, cp, N_RW_PAD), lambda b, c: (b, c, 0)),
                  pl.BlockSpec((1, 1, N_RW_PAD), lambda b, c: (b, 0, 0)),
                  full((1, N_RW_PAD)), full((1, RW_WIDTH)), full((1, RW_WIDTH)),
                  full((LANES, RW_WIDTH)), full((LANES, RW_WIDTH)), full((G_PAD, RW_WIDTH))],
        out_specs=[out_blk] * 6,
        out_shape=[out_sds] * 6,
        scratch_shapes=[pltpu.VMEM((1, N_RW_PAD), F32)],
        compiler_params=_cp("parallel", "arbitrary"),
        name="rwkv_prep",
    )(hrw3, sh0, mu, w0, a0, wup, aup, gup)


def _split(x):
    hi = x.astype(BF16)
    lo = (x - hi.astype(F32)).astype(BF16)
    return hi, lo


def _mm3(a, b, dot=_dot):
    return dot(a[1], b[0]) + dot(a[0], b[1]) + dot(a[0], b[0])


def _mm_exact_rhs(x, m):
    hi, lo = _split(x)
    return _dot(lo, m) + _dot(hi, m)


def _rw_chunk(args, masks, t_valid):
    C = args[0][0].shape[0]
    incl_b, strict, incl, eye, seg_b, head_a, blockdiag, diag, levels = masks
    head_masks = (head_a, jnp.logical_not(head_a))
    P = range(len(args))
    PH = [(p, hd) for p in P for hd in range(2)]

    pre = []
    for (r, k, v, lw, a, g, kkp, kap, rkp, lnw, lnb, H) in args:
        if t_valid < C:
            valid = lax.broadcasted_iota(jnp.int32, (C, LANES), 0) < t_valid
            lw = jnp.where(valid, lw, 0.0)
            k = jnp.where(valid, k, 0.0)
            v = jnp.where(valid, v, 0.0)
        kk = k * kkp
        kk = kk * lax.rsqrt(jnp.maximum(_mm_exact_rhs(kk * kk, seg_b), 1e-24))
        k2 = k * (1.0 + (a - 1.0) * kap)
        b = kk * a
        lw_hi, lw_lo = _split(lw)
        lw_lo2 = (lw - lw_hi.astype(F32) - lw_lo.astype(F32)).astype(BF16)
        gam = _dot(incl_b, lw_lo2) + _dot(incl_b, lw_lo) + _dot(incl_b, lw_hi)
        gl = gam[C - 1:C, :]
        rt = r * jnp.exp(gam)
        at = -kk * jnp.exp(gam - lw)
        ginv = jnp.exp(-gam)
        gh = jnp.exp(gl - gam)
        pre.append(dict(r=r, v=v, k2=k2, g=g, rkp=rkp, lnw=lnw, lnb=lnb, H=H, gl=gl, rt=rt, at=at,
                        bt=_split(b * ginv), kt=_split(k2 * ginv), bh=b * gh, kh=k2 * gh,
                        Hs=_split(H), vs=_split(v)))

    stacked = C % LANES == 0
    cat0 = lambda xs: jnp.concatenate(xs, axis=0)
    if stacked:
        ar = [cat0([pre[p]['at'], pre[p]['rt']]) for p in P]
        bk = [tuple(cat0([pre[p]['bt'][i], pre[p]['kt'][i]]) for i in range(2)) for p in P]
        head2 = [cat0([hm, hm]) for hm in head_masks]
        A = {(p, hd): _mm3(_split(jnp.where(head2[hd], ar[p], 0.0)), bk[p], _dot_nt) for p, hd in PH}
        aab = {c: jnp.where(strict, A[c][:C, :C], 0.0) for c in PH}
        aak = {c: jnp.where(strict, A[c][:C, C:], 0.0) for c in PH}
        arb = {c: jnp.where(incl, A[c][C:, :C], 0.0) for c in PH}
        ark = {c: jnp.where(incl, A[c][C:, C:], 0.0) for c in PH}
        XR = [_mm3(_split(ar[p]), pre[p]['Hs']) for p in P]
        X = [xr[:C] for xr in XR]
        RH = [xr[C:] for xr in XR]
    else:
        atm = {(p, hd): _split(jnp.where(head_masks[hd], pre[p]['at'], 0.0)) for p, hd in PH}
        rtm = {(p, hd): _split(jnp.where(head_masks[hd], pre[p]['rt'], 0.0)) for p, hd in PH}
        aab = {c: jnp.where(strict, _mm3(atm[c], pre[c[0]]['bt'], _dot_nt), 0.0) for c in PH}
        aak = {c: jnp.where(strict, _mm3(atm[c], pre[c[0]]['kt'], _dot_nt), 0.0) for c in PH}
        arb = {c: jnp.where(incl, _mm3(rtm[c], pre[c[0]]['bt'], _dot_nt), 0.0) for c in PH}
        ark = {c: jnp.where(incl, _mm3(rtm[c], pre[c[0]]['kt'], _dot_nt), 0.0) for c in PH}
        X = [_mm3(_split(pre[p]['at']), pre[p]['Hs']) for p in P]
        RH = [_mm3(_split(pre[p]['rt']), pre[p]['Hs']) for p in P]
    tinv = {c: eye + jnp.where(levels[0], aab[c], 0.0) for c in PH}
    for lvl in levels[1:]:
        ts = {c: _split(tinv[c]) for c in PH}
        mid = {c: _split(_mm3(_split(jnp.where(lvl, aab[c], 0.0)), ts[c])) for c in PH}
        tinv = {c: tinv[c] + _mm3(ts[c], mid[c]) for c in PH}
    if stacked:
        ys2 = [_mm3(_split(cat0([aak[(p, 0)], aak[(p, 1)]])), pre[p]['vs']) for p in P]
        ysum = [_split(X[p] + jnp.where(head_a, ys2[p][:C], ys2[p][C:])) for p in P]
        u2 = [_mm3(_split(cat0([tinv[(p, 0)], tinv[(p, 1)]])), ysum[p]) for p in P]
        U = [jnp.where(head_a, u2[p][:C], u2[p][C:]) for p in P]
    else:
        ys = {c: _mm3(_split(aak[c]), pre[c[0]]['vs']) for c in PH}
        ysum = [_split(X[p] + jnp.where(head_a, ys[(p, 0)], ys[(p, 1)])) for p in P]
        u = {c: _mm3(_split(tinv[c]), ysum[c[0]]) for c in PH}
        U = [jnp.where(head_a, u[(p, 0)], u[(p, 1)]) for p in P]

    hn, uvh = [], []
    for p in P:
        dg = jnp.where(diag, jnp.exp(pre[p]['gl']), 0.0)
        lhs = _split(cat0([pre[p]['bh'], pre[p]['kh'], dg]))
        uvh.append(_split(cat0([U[p], pre[p]['v'], pre[p]['H']])))
        hn.append(jnp.where(blockdiag, _mm3(lhs, uvh[p], _dot_tn), 0.0))

    if stacked:
        both = lambda p, hd: jnp.concatenate([arb[(p, hd)], ark[(p, hd)]], axis=1)
        dd2 = [_mm3(_split(cat0([both(p, 0), both(p, 1)])), tuple(t[:2 * C] for t in uvh[p])) for p in P]
        dsel = [jnp.where(head_a, dd2[p][:C], dd2[p][C:]) for p in P]
    else:
        Us = [_split(U[p]) for p in P]
        dd = {c: _mm3(_split(arb[c]), Us[c[0]]) + _mm3(_split(ark[c]), pre[c[0]]['vs']) for c in PH}
        dsel = [jnp.where(head_a, dd[(p, 0)], dd[(p, 1)]) for p in P]
    inv_n = 1.0 / RW_HEAD
    outs = []
    for p in P:
        q = pre[p]
        O = RH[p] + dsel[p]
        mean = _mm_exact_rhs(O, seg_b) * inv_n
        d = O - mean
        var = _mm_exact_rhs(d * d, seg_b) * inv_n
        on = d * lax.rsqrt(var + LNX_EPS) * q['lnw'] + q['lnb']
        bonus = _mm_exact_rhs(q['r'] * q['k2'] * q['rkp'], seg_b) * q['v']
        outs.append(((on + bonus) * q['g'], hn[p]))
    return outs


def _rw_scan_kernel(r_ref, k_ref, v_ref, lw_ref, a_ref, g_ref, kk_ref, ka_ref, rk_ref,
                    lnw_ref, lnb_ref, h0_ref, o_ref, hout_ref, h_scr, *, t_valid):
    C = r_ref.shape[1]
    n_pairs = r_ref.shape[2] // LANES
    ci = pl.program_id(2)

    @pl.when(ci == 0)
    def _():
        h_scr[...] = h0_ref[0]

    ti = lax.broadcasted_iota(jnp.int32, (C, C), 0)
    si = lax.broadcasted_iota(jnp.int32, (C, C), 1)
    hr = lax.broadcasted_iota(jnp.int32, (LANES, LANES), 0)
    hc = lax.broadcasted_iota(jnp.int32, (LANES, LANES), 1)
    blockdiag = (hr // RW_HEAD) == (hc // RW_HEAD)
    levels = []
    m = 1
    while m < C:
        levels.append(((ti // (2 * m)) == (si // (2 * m))) & ((ti % (2 * m)) >= m) & ((si % (2 * m)) < m))
        m *= 2
    masks = ((si <= ti).astype(BF16), si < ti, si <= ti, (si == ti).astype(F32),
             blockdiag.astype(BF16),
             lax.broadcasted_iota(jnp.int32, (C, LANES), 1) < RW_HEAD, blockdiag, hr == hc, levels)
    sls = [slice(LANES * pi, LANES * (pi + 1)) for pi in range(n_pairs)]
    args = [(r_ref[0, :, sl], k_ref[0, :, sl], v_ref[0, :, sl], lw_ref[0, :, sl], a_ref[0, :, sl],
             g_ref[0, :, sl], kk_ref[:, sl], ka_ref[:, sl], rk_ref[:, sl], lnw_ref[:, sl], lnb_ref[:, sl],
             h_scr[pi]) for pi, sl in enumerate(sls)]
    res = _rw_chunk(args, masks, t_valid)
    for pi, sl in enumerate(sls):
        o_ref[0, :, sl] = res[pi][0].astype(o_ref.dtype)
        h_scr[pi] = res[pi][1]

    @pl.when(ci == pl.num_programs(2) - 1)
    def _():
        hout_ref[0] = h_scr[...]


def _rw_scan(r, k, v, lw, a, g, kk, ka, rk, lnw, lnb, h0, t_valid, out_dtype):
    B, T, _ = r.shape
    C = min(T, 128)
    pb = N_RW_PAIRS
    w = LANES * pb
    tok = pl.BlockSpec((1, C, w), lambda b, p, c: (b, c, p))
    vec = pl.BlockSpec((1, w), lambda b, p, c: (0, p))
    st = pl.BlockSpec((1, pb, LANES, LANES), lambda b, p, c: (b, p, 0, 0))
    return pl.pallas_call(
        functools.partial(_rw_scan_kernel, t_valid=t_valid),
        grid=(B, N_RW_PAIRS // pb, T // C),
        in_specs=[tok] * 6 + [vec] * 5 + [st],
        out_specs=[tok, st],
        out_shape=[jax.ShapeDtypeStruct((B, T, RW_WIDTH), out_dtype),
                   jax.ShapeDtypeStruct((B, N_RW_PAIRS, LANES, LANES), F32)],
        scratch_shapes=[pltpu.VMEM((pb, LANES, LANES), F32)],
        compiler_params=_cp("parallel", "parallel", "arbitrary"),
        name="rwkv_scan",
    )(r, k, v, lw, a, g, kk, ka, rk, lnw, lnb, h0)


def _out_proj_kernel(a1_ref, a2_ref, w_ref, res_ref, o_ref):
    acc = _dot(a1_ref[...].astype(BF16), w_ref[:DA_WIDTH, :])
    acc = acc + _dot(a2_ref[...].astype(BF16), w_ref[DA_WIDTH:, :])
    o_ref[...] = res_ref[...] + acc


def _out_proj(a1, a2, w, res):
    M, D = res.shape
    tm = min(M, 1024)
    tn = 512
    return pl.pallas_call(
        _out_proj_kernel,
        grid=(M // tm, D // tn),
        in_specs=[pl.BlockSpec((tm, DA_WIDTH), lambda i, j: (i, 0)),
                  pl.BlockSpec((tm, RW_WIDTH), lambda i, j: (i, 0)),
                  pl.BlockSpec((DA_WIDTH + RW_WIDTH, tn), lambda i, j: (0, j)),
                  pl.BlockSpec((tm, tn), lambda i, j: (i, j))],
        out_specs=pl.BlockSpec((tm, tn), lambda i, j: (i, j)),
        out_shape=jax.ShapeDtypeStruct((M, D), F32),
        compiler_params=_cp("parallel", "parallel"),
        name="out_proj",
    )(a1, a2, w, res)


def _ffn_up_kernel(x_ref, g_ref, wg_ref, wu_ref, o_ref, xn_ref):
    @pl.when(pl.program_id(1) == 0)
    def _():
        x = x_ref[...]
        ms = jnp.mean(x * x, axis=-1, keepdims=True)
        xn_ref[...] = (x * lax.rsqrt(ms + NORM_EPS) * g_ref[...]).astype(BF16)

    xn = xn_ref[...]
    gate = _dot(xn, wg_ref[...])
    up = _dot(xn, wu_ref[...])
    o_ref[...] = (gate * jax.nn.sigmoid(gate) * up).astype(o_ref.dtype)


def _ffn_up(x, g, wg, wu):
    M, D = x.shape
    tm = min(M, 1024)
    tn = 512
    return pl.pallas_call(
        _ffn_up_kernel,
        grid=(M // tm, D_FF // tn),
        in_specs=[pl.BlockSpec((tm, D), lambda i, j: (i, 0)),
                  pl.BlockSpec((1, D), lambda i, j: (0, 0)),
                  pl.BlockSpec((D, tn), lambda i, j: (0, j)),
                  pl.BlockSpec((D, tn), lambda i, j: (0, j))],
        out_specs=pl.BlockSpec((tm, tn), lambda i, j: (i, j)),
        out_shape=jax.ShapeDtypeStruct((M, D_FF), BF16),
        scratch_shapes=[pltpu.VMEM((tm, D), BF16)],
        compiler_params=_cp("parallel", "arbitrary"),
        name="ffn_gate_up",
    )(x, g, wg, wu)


def _ffn_down_kernel(a_ref, w_ref, res_ref, o_ref):
    o_ref[...] = res_ref[...] + _dot(a_ref[...], w_ref[...])


def _ffn_down(a, w, res):
    M, D = res.shape
    tm = min(M, 1024)
    tn = 512
    return pl.pallas_call(
        _ffn_down_kernel,
        grid=(M // tm, D // tn),
        in_specs=[pl.BlockSpec((tm, D_FF), lambda i, j: (i, 0)),
                  pl.BlockSpec((D_FF, tn), lambda i, j: (0, j)),
                  pl.BlockSpec((tm, tn), lambda i, j: (i, j))],
        out_specs=pl.BlockSpec((tm, tn), lambda i, j: (i, j)),
        out_shape=jax.ShapeDtypeStruct((M, D), F32),
        compiler_params=_cp("parallel", "parallel"),
        name="ffn_down",
    )(a, w, res)


def _rope_tables(pos):
    half = DA_QK_DIM // 2
    inv_freq = jnp.power(ROPE_THETA, -jnp.arange(half, dtype=F32) / half)
    ang = pos.astype(F32)[:, None] * inv_freq[None, :]
    cos = jnp.cos(ang)
    sin = jnp.sin(ang)
    reps = LANES // DA_QK_DIM
    cos_t = jnp.tile(jnp.concatenate([cos, cos], axis=-1), (1, reps))
    sin_t = jnp.tile(jnp.concatenate([-sin, sin], axis=-1), (1, reps))
    return cos_t, sin_t


def _state_to_pairs(S):
    B = S.shape[0]
    St = jnp.swapaxes(S, -1, -2).reshape(B, N_RW_PAIRS, 2, RW_HEAD, RW_HEAD)
    eye2 = jnp.eye(2, dtype=S.dtype)
    Hp = St[:, :, :, :, None, :] * eye2[None, None, :, None, :, None]
    return Hp.reshape(B, N_RW_PAIRS, LANES, LANES)


def _pairs_to_state(Hp):
    B = Hp.shape[0]
    H5 = Hp.reshape(B, N_RW_PAIRS, 2, RW_HEAD, 2, RW_HEAD)
    diag = jnp.stack([H5[:, :, 0, :, 0, :], H5[:, :, 1, :, 1, :]], axis=2)
    return jnp.swapaxes(diag.reshape(B, N_RW_HEADS, RW_HEAD, RW_HEAD), -1, -2)


def _decode_query_rows(qb):
    Bd = qb.shape[0]
    sub = jnp.arange(DA_WIDTH) // DA_QK_DIM
    want = 2 * jnp.arange(N_DA_HEADS)[:, None] + jnp.arange(2)[None, :]
    mask = sub[None, None, :] == want[:, :, None]
    rows = jnp.where(mask[None, :, :, None, :], qb[:, None, None, :, :], jnp.zeros((), qb.dtype))
    return rows.reshape(Bd, -1, DA_WIDTH)


def _decode_head_rows(o, n_new):
    Bd = o.shape[0]
    o5 = o.reshape(Bd, N_DA_HEADS, 2, n_new, DA_V_DIM)[:, :, 0]
    return jnp.transpose(o5, (0, 2, 1, 3)).reshape(Bd, n_new, DA_WIDTH)


def _pad_rows(x, n):
    return jnp.pad(x, ((0, 0), (0, n - x.shape[1]), (0, 0)))


def _layer(x3, cos, sin, S0, shift0, lam, lam_init, p, decode=None):
    B, T, D = x3.shape
    M = B * T
    x = x3.reshape(M, D)
    hda = _norm_mm(x, p['norm_mix'], p['w_da'], 1024)
    hrw = _norm_mm(x, p['norm_mix'], p['w_rw'], 1152)
    qb, kf, kb, vb = _qk_post(hda, p['q_gain'], p['k_gain'], cos, sin, T)
    hda3 = hda.reshape(B, T, 3 * DA_WIDTH)
    v_new = hda3[:, :, 2 * DA_WIDTH:]
    if decode is None:
        o_da = _attn_prompt(lam, qb.reshape(B, T, DA_WIDTH), kb.reshape(B, T, DA_WIDTH),
                            vb.reshape(B, T, DA_WIDTH), p['subln_gain'], lam_init)
    else:
        page_table, cache_kt, cache_v4, layer = decode
        v_new_h = jnp.transpose(v_new.reshape(B, T, N_DA_HEADS, DA_V_DIM), (0, 2, 1, 3))
        v_new_h = jnp.pad(v_new_h, ((0, 0), (0, 0), (0, NEW_PAD - T), (0, 0)))
        o_da = _attn_sample(page_table, lam, _decode_query_rows(qb.reshape(B, T, DA_WIDTH)),
                            _pad_rows(kf.reshape(B, T, DA_WIDTH), NEW_PAD), v_new_h, p['subln_gain'],
                            cache_kt, cache_v4, layer, lam_init)
        o_da = _decode_head_rows(o_da, T)
    hrw3 = hrw.reshape(B, T, N_RW_PAD)
    shift_new = hrw3[:, -1, :N_RW_COLS]
    t_pad = max(T, 8)
    hrw3p = _pad_rows(hrw3, t_pad)
    sh0 = jnp.pad(shift0, ((0, 0), (0, N_RW_PAD - N_RW_COLS)))[:, None, :]
    r, k, v, lw, a, g = _rw_prep(hrw3p, sh0, p['mu_shift'], p['w0'], p['a0'], p['w_up'], p['a_up'],
                                 p['g_up'])
    o_rw, hp = _rw_scan(r, k, v, lw, a, g, p['k_k'], p['k_a'], p['r_k'], p['lnx_w'], p['lnx_b'],
                        _state_to_pairs(S0), T, BF16 if decode is None else F32)
    o_rw = o_rw[:, :T]
    S_new = _pairs_to_state(hp)
    x1 = _out_proj(o_da.reshape(M, DA_WIDTH), o_rw.reshape(M, RW_WIDTH), p['w_out'], x)
    ff = _ffn_up(x1, p['norm_ffn'], p['w_gate'], p['w_up_ffn'])
    x2 = _ffn_down(ff, p['w_down'], x1)
    return x2.reshape(B, T, D), kf, v_new, S_new, shift_new


def kernel(x_prompt, x_sample, cache_k, cache_v, state_wkv, state_shift, page_table, norm_mix, w_in, q_gain, k_gain, lambda_q1, lambda_k1, lambda_q2, lambda_k2, subln_gain, mu_shift, w0, w_up, a0, a_up, g_up, k_k, k_a, r_k, lnx_w, lnx_b, w_out, norm_ffn, w_gate, w_up_ffn, w_down):
    B, S, _ = x_prompt.shape
    Bd, T, _ = x_sample.shape
    depth = w_in.shape[0]
    n_pages = page_table.shape[1]
    past_len = n_pages * PAGE_SIZE
    cos_p, sin_p = _rope_tables(jnp.arange(S, dtype=jnp.int32))
    cos_s, sin_s = _rope_tables(past_len + jnp.arange(T, dtype=jnp.int32))
    cos_s = jnp.tile(cos_s, (Bd, 1))
    sin_s = jnp.tile(sin_s, (Bd, 1))
    n_pool = cache_k.shape[1]
    cache_kt = jnp.transpose(cache_k, (0, 1, 3, 4, 2)).reshape(depth, n_pool, DA_WIDTH, PAGE_SIZE)
    cache_v4 = cache_v.reshape(depth, n_pool, PAGE_SIZE * N_DA_HEADS, DA_V_DIM)
    row = lambda t: t.reshape(1, -1)
    tile_gain = lambda t: jnp.tile(t, LANES // DA_QK_DIM).reshape(1, LANES)

    xp, xs = x_prompt, x_sample
    outs = [[] for _ in range(8)]
    for l in range(depth):
        lam_init = 0.8 - 0.6 * math.exp(-0.3 * l)
        lam = (jnp.exp(jnp.sum(lambda_q1[l] * lambda_k1[l])) - jnp.exp(jnp.sum(lambda_q2[l] * lambda_k2[l]))
               + lam_init).reshape(1).astype(F32)
        w_in_b = w_in[l].astype(BF16)
        p = {
            'norm_mix': row(norm_mix[l]),
            'w_da': w_in_b[:, :3 * DA_WIDTH],
            'w_rw': jnp.pad(w_in_b[:, 3 * DA_WIDTH:], ((0, 0), (0, N_RW_PAD - N_RW_COLS))),
            'q_gain': tile_gain(q_gain[l]), 'k_gain': tile_gain(k_gain[l]),
            'subln_gain': row(subln_gain[l]),
            'mu_shift': jnp.pad(row(mu_shift[l]), ((0, 0), (0, N_RW_PAD - N_RW_COLS))),
            'w0': row(w0[l]), 'a0': row(a0[l]),
            'w_up': jnp.pad(w_up[l], ((0, LANES - W_LORA), (0, 0))),
            'a_up': jnp.pad(a_up[l], ((W_LORA, LANES - W_LORA - A_LORA), (0, 0))),
            'g_up': jnp.pad(g_up[l], ((0, G_PAD - G_LORA), (0, 0))),
            'k_k': row(k_k[l]), 'k_a': row(k_a[l]), 'r_k': row(r_k[l]),
            'lnx_w': row(lnx_w[l]), 'lnx_b': row(lnx_b[l]),
            'w_out': w_out[l].astype(BF16), 'norm_ffn': row(norm_ffn[l]),
            'w_gate': w_gate[l].astype(BF16), 'w_up_ffn': w_up_ffn[l].astype(BF16),
            'w_down': w_down[l].astype(BF16),
        }
        S0_p = jnp.zeros((B, N_RW_HEADS, RW_HEAD, RW_HEAD), F32)
        sh0_p = jnp.zeros((B, N_RW_COLS), F32)
        xp, k_new, v_new, S_new, sh_new = _layer(xp, cos_p, sin_p, S0_p, sh0_p, lam, lam_init, p)
        outs[0].append(k_new.reshape(B, S // PAGE_SIZE, PAGE_SIZE, N_DA_SUB, DA_QK_DIM))
        outs[1].append(v_new.reshape(B, S // PAGE_SIZE, PAGE_SIZE, N_DA_HEADS, DA_V_DIM))
        outs[2].append(S_new)
        outs[3].append(sh_new)
        xs, k_new, v_new, S_new, sh_new = _layer(xs, cos_s, sin_s, state_wkv[l], state_shift[l], lam,
                                                 lam_init, p, decode=(page_table, cache_kt, cache_v4, l))
        outs[4].append(k_new.reshape(Bd, T, N_DA_SUB, DA_QK_DIM))
        outs[5].append(v_new.reshape(Bd, T, N_DA_HEADS, DA_V_DIM))
        outs[6].append(S_new)
        outs[7].append(sh_new)
    return (xp, xs) + tuple(jnp.stack(o) for o in outs)
```

```python
import functools
import math

import jax
import jax.numpy as jnp
from jax import lax
from jax.experimental import pallas as pl
from jax.experimental.pallas import tpu as pltpu

F32 = jnp.float32
BF16 = jnp.bfloat16

D_MODEL = 2048
PAGE_SIZE = 128
DA_WIDTH = 1024
RW_WIDTH = 1024
DA_V_DIM = 128
N_DA_HEADS = 8
DA_QK_DIM = 64
N_DA_SUB = 16
RW_HEAD = 64
N_RW_HEADS = 16
N_RW_PAIRS = N_RW_HEADS // 2
W_LORA = 64
A_LORA = 64
G_LORA = 160
N_RW_COLS = 3 * RW_WIDTH + W_LORA + A_LORA + G_LORA
LANES = 128
N_RW_PAD = ((N_RW_COLS + LANES - 1) // LANES) * LANES
G_PAD = N_RW_PAD - 3 * RW_WIDTH - LANES
D_FF = 5632
ROPE_THETA = 10000.0
NORM_EPS = 1e-6
LNX_EPS = 64e-5
MASK_VALUE = -1e30
DA_SCALE = DA_QK_DIM ** -0.5
VMEM_LIMIT_BYTES = 56 * 1024 * 1024
NEW_PAD = 16


def _cp(*sem):
    return pltpu.CompilerParams(dimension_semantics=sem, vmem_limit_bytes=VMEM_LIMIT_BYTES)


def _dot(a, b, precision=None):
    return jnp.dot(a, b, preferred_element_type=F32, precision=precision)


def _dot_nt(a, b, precision=None):
    return lax.dot_general(a, b, (((1,), (1,)), ((), ())), preferred_element_type=F32,
                           precision=precision)


def _dot_tn(a, b, precision=None):
    return lax.dot_general(a, b, (((0,), (0,)), ((), ())), preferred_element_type=F32,
                           precision=precision)


def _seg_ones(n, seg):
    r = lax.broadcasted_iota(jnp.int32, (n, n), 0) // seg
    c = lax.broadcasted_iota(jnp.int32, (n, n), 1) // seg
    return (r == c).astype(F32)


def _norm_mm_kernel(x_ref, g_ref, w_ref, o_ref, xn_ref):
    @pl.when(pl.program_id(1) == 0)
    def _():
        x = x_ref[...]
        ms = jnp.mean(x * x, axis=-1, keepdims=True)
        xn_ref[...] = (x * lax.rsqrt(ms + NORM_EPS) * g_ref[...]).astype(BF16)

    o_ref[...] = _dot(xn_ref[...], w_ref[...])


def _norm_mm(x, g, w, tn):
    M, D = x.shape
    N = w.shape[1]
    tm = min(M, 1024)
    return pl.pallas_call(
        _norm_mm_kernel,
        grid=(M // tm, N // tn),
        in_specs=[pl.BlockSpec((tm, D), lambda i, j: (i, 0)),
                  pl.BlockSpec((1, D), lambda i, j: (0, 0)),
                  pl.BlockSpec((D, tn), lambda i, j: (0, j))],
        out_specs=pl.BlockSpec((tm, tn), lambda i, j: (i, j)),
        out_shape=jax.ShapeDtypeStruct((M, N), F32),
        scratch_shapes=[pltpu.VMEM((tm, D), BF16)],
        compiler_params=_cp("parallel", "arbitrary"),
        name="norm_in_proj",
    )(x, g, w)


def _qk_post_kernel(hq_ref, hk_ref, hv_ref, qg_ref, kg_ref, cos_ref, sin_ref, qb_ref, kf_ref, kb_ref, vb_ref):
    tm = hq_ref.shape[0]
    vb_ref[...] = hv_ref[...].astype(BF16)
    lane = lax.broadcasted_iota(jnp.int32, (tm, LANES), 1)
    first = (lane % DA_QK_DIM) < (DA_QK_DIM // 2)
    seg = (_seg_ones(LANES, DA_QK_DIM) * (1.0 / DA_QK_DIM)).astype(BF16)
    cos = cos_ref[...]
    sin = sin_ref[...]

    def proc(x, g):
        ms = _mm_exact_rhs(x * x, seg)
        y = x * lax.rsqrt(ms + NORM_EPS) * g
        partner = jnp.where(first, pltpu.roll(y, LANES - DA_QK_DIM // 2, 1),
                            pltpu.roll(y, DA_QK_DIM // 2, 1))
        return y * cos + partner * sin

    for c in range(DA_WIDTH // LANES):
        sl = slice(LANES * c, LANES * (c + 1))
        q = proc(hq_ref[:, sl], qg_ref[...])
        qb_ref[:, sl] = (q * DA_SCALE).astype(BF16)
        k = proc(hk_ref[:, sl], kg_ref[...])
        kf_ref[:, sl] = k
        kb_ref[:, sl] = k.astype(BF16)


def _qk_post(hda, qg, kg, cos, sin, T):
    M = hda.shape[0]
    tm = min(M, 512)
    if cos.shape[0] == M:
        tab_map = lambda i: (i, 0)
    else:
        nper = T // tm
        tab_map = lambda i: (i % nper, 0)
    blk = lambda c: pl.BlockSpec((tm, DA_WIDTH), lambda i: (i, c))
    vec = pl.BlockSpec((1, LANES), lambda i: (0, 0))
    tab = pl.BlockSpec((tm, LANES), tab_map)
    return pl.pallas_call(
        _qk_post_kernel,
        grid=(M // tm,),
        in_specs=[blk(0), blk(1), blk(2), vec, vec, tab, tab],
        out_specs=[blk(0), blk(0), blk(0), blk(0)],
        out_shape=[jax.ShapeDtypeStruct((M, DA_WIDTH), BF16),
                   jax.ShapeDtypeStruct((M, DA_WIDTH), F32),
                   jax.ShapeDtypeStruct((M, DA_WIDTH), BF16),
                   jax.ShapeDtypeStruct((M, DA_WIDTH), BF16)],
        compiler_params=_cp("parallel"),
        name="qk_norm_rope",
    )(hda, hda, hda, qg, kg, cos, sin)


def _subln(o, g, lam_init):
    ms = jnp.mean(o * o, axis=-1, keepdims=True)
    return o * lax.rsqrt(ms + NORM_EPS) * g * (1.0 - lam_init)


def _attn_kernel(lam_ref, q_ref, k_ref, v_ref, g_ref, o_ref, m_scr, l_scr, acc_scr, *, tq, tk, lam_init):
    qi = pl.program_id(2)
    n_heads = q_ref.shape[2] // LANES
    lane = lax.broadcasted_iota(jnp.int32, (tq, LANES), 1)
    chains = [(hd, mi) for hd in range(n_heads) for mi in range(2)]
    qs = []
    for hd, mi in chains:
        q = q_ref[0, :, LANES * hd:LANES * (hd + 1)]
        keep = (lane < DA_QK_DIM) if mi == 0 else (lane >= DA_QK_DIM)
        qs.append(jnp.where(keep, q, jnp.zeros_like(q)))
    m_scr[...] = jnp.full(m_scr.shape, MASK_VALUE, F32)
    l_scr[...] = jnp.zeros(l_scr.shape, F32)
    acc_scr[...] = jnp.zeros(acc_scr.shape, F32)

    def block(j, masked):
        off = pl.multiple_of(j * tk, tk)
        kbs = [k_ref[0, pl.ds(off, tk), LANES * hd:LANES * (hd + 1)] for hd in range(n_heads)]
        vbs = [v_ref[0, pl.ds(off, tk), LANES * hd:LANES * (hd + 1)] for hd in range(n_heads)]
        nc = range(len(chains))
        ms = [m_scr[c] for c in nc]
        ls = [l_scr[c] for c in nc]
        accs = [acc_scr[c] for c in nc]
        ss = [_dot_nt(qs[c], kbs[chains[c][0]]) for c in nc]
        if masked:
            row = qi * tq + lax.broadcasted_iota(jnp.int32, (tq, tk), 0)
            col = off + lax.broadcasted_iota(jnp.int32, (tq, tk), 1)
            ss = [jnp.where(col <= row, s, MASK_VALUE) for s in ss]
        mns = [jnp.maximum(ms[c], jnp.max(ss[c], axis=-1, keepdims=True)) for c in nc]
        als = [jnp.exp(ms[c] - mns[c]) for c in nc]
        ps = [jnp.exp(ss[c] - jnp.concatenate([mns[c]] * (tk // LANES), axis=1)) for c in nc]
        sums = [functools.reduce(lambda a, b: a + b, [ps[c][:, LANES * i:LANES * (i + 1)]
                                                      for i in range(tk // LANES)]) for c in nc]
        pvs = [_dot(ps[c].astype(BF16), vbs[chains[c][0]]) for c in nc]
        for c in nc:
            m_scr[c] = mns[c]
            l_scr[c] = ls[c] * als[c] + sums[c]
            acc_scr[c] = accs[c] * als[c] + pvs[c]

    n_full = (qi * tq) // tk

    def body(j, c):
        block(j, False)
        return c

    lax.fori_loop(0, n_full, body, 0)
    block(n_full, True)
    for hd in range(n_heads):
        c1, c2 = 2 * hd, 2 * hd + 1
        l1 = jnp.sum(l_scr[c1], axis=-1, keepdims=True)
        l2 = jnp.sum(l_scr[c2], axis=-1, keepdims=True)
        o = acc_scr[c1] / l1 - lam_ref[0] * (acc_scr[c2] / l2)
        o_ref[0, :, LANES * hd:LANES * (hd + 1)] = _subln(o, g_ref[...], lam_init).astype(o_ref.dtype)


def _attn_prompt(lam, qb, kb, vb, g, lam_init):
    B, S, _ = qb.shape
    tq = min(S, 128)
    tk = min(S, 256)
    hb = 2
    w = LANES * hb
    return pl.pallas_call(
        functools.partial(_attn_kernel, tq=tq, tk=tk, lam_init=lam_init),
        grid=(B, N_DA_HEADS // hb, S // tq),
        in_specs=[pl.BlockSpec(memory_space=pltpu.SMEM),
                  pl.BlockSpec((1, tq, w), lambda b, h, i: (b, i, h)),
                  pl.BlockSpec((1, S, w), lambda b, h, i: (b, 0, h)),
                  pl.BlockSpec((1, S, w), lambda b, h, i: (b, 0, h)),
                  pl.BlockSpec((1, DA_V_DIM), lambda b, h, i: (0, 0))],
        out_specs=pl.BlockSpec((1, tq, w), lambda b, h, i: (b, i, h)),
        out_shape=jax.ShapeDtypeStruct((B, S, DA_WIDTH), BF16),
        scratch_shapes=[pltpu.VMEM((2 * hb, tq, LANES), F32), pltpu.VMEM((2 * hb, tq, LANES), F32),
                        pltpu.VMEM((2 * hb, tq, DA_V_DIM), F32)],
        compiler_params=_cp("parallel", "parallel", "parallel"),
        name="diff_attn_prompt",
    )(lam, qb, kb, vb, g)


def _dec_attn_kernel(pt_ref, lam_ref, q_ref, kn_ref, vn_ref, g_ref, *rest, pp, n_new, lam_init):
    k_refs = rest[:pp]
    v_refs = rest[pp:2 * pp]
    o_ref = rest[2 * pp]
    m_scr, l_scr, acc_scr = rest[2 * pp + 1:]
    p_id = pl.program_id(1)
    rows_h = 2 * n_new

    @pl.when(p_id == 0)
    def _():
        m_scr[...] = jnp.full(m_scr.shape, MASK_VALUE, F32)
        l_scr[...] = jnp.zeros(l_scr.shape, F32)
        acc_scr[...] = jnp.zeros(acc_scr.shape, F32)

    q = q_ref[0]

    def update(s, v_of_head):
        m = m_scr[...]
        mn = jnp.maximum(m, jnp.max(s, axis=-1, keepdims=True))
        al = jnp.exp(m - mn)
        p = jnp.exp(s - mn)
        l_scr[...] = l_scr[...] * al + jnp.sum(p, axis=-1, keepdims=True)
        pv = [_dot(p[rows_h * h:rows_h * (h + 1)].astype(BF16), v_of_head(h)) for h in range(N_DA_HEADS)]
        acc_scr[...] = acc_scr[...] * al + jnp.concatenate(pv, axis=0)
        m_scr[...] = mn

    def page_values(h):
        return jnp.concatenate([v_refs[i].at[0, 0][pl.ds(h, PAGE_SIZE, stride=N_DA_HEADS), :].astype(BF16)
                                for i in range(pp)], axis=0)

    s = jnp.concatenate([_dot(q, k_refs[i][0, 0].astype(BF16)) for i in range(pp)], axis=1)
    update(s, page_values)

    @pl.when(p_id == pl.num_programs(1) - 1)
    def _():
        s = _dot_nt(q, kn_ref[0].astype(BF16))
        r = lax.broadcasted_iota(jnp.int32, s.shape, 0)
        t = lax.broadcasted_iota(jnp.int32, s.shape, 1)
        s = jnp.where(t <= (r % n_new), s, MASK_VALUE)
        update(s, lambda h: vn_ref[0, h].astype(BF16))
        n = acc_scr[...] / l_scr[...]
        comb = n - lam_ref[0] * pltpu.roll(n, n.shape[0] - n_new, 0)
        o_ref[0] = _subln(comb, g_ref[...], lam_init)


def _attn_sample(page_table, lam, qrows, k_new, v_new_h, g, cache_kt, cache_v4, layer, lam_init, pp=8):
    Bd, nrow, _ = qrows.shape
    n_new = nrow // (2 * N_DA_HEADS)
    n_pages = page_table.shape[1]
    pt = page_table.reshape(-1)

    def page_spec(i):
        return pl.BlockSpec((1, 1, DA_WIDTH, LANES),
                            lambda b, p, pt_ref: (layer, pt_ref[b * n_pages + p * pp + i], 0, 0))

    per_b = lambda shape: pl.BlockSpec((1,) + shape, lambda b, p, pt_ref: (b,) + (0,) * len(shape))
    grid_spec = pltpu.PrefetchScalarGridSpec(
        num_scalar_prefetch=1,
        grid=(Bd, n_pages // pp),
        in_specs=[pl.BlockSpec(memory_space=pltpu.SMEM), per_b((nrow, DA_WIDTH)), per_b(k_new.shape[1:]),
                  per_b(v_new_h.shape[1:]), pl.BlockSpec((1, DA_V_DIM), lambda b, p, pt_ref: (0, 0))]
                 + [page_spec(i) for i in range(pp)] + [page_spec(i) for i in range(pp)],
        out_specs=per_b((nrow, DA_V_DIM)),
        scratch_shapes=[pltpu.VMEM((nrow, 1), F32), pltpu.VMEM((nrow, 1), F32),
                        pltpu.VMEM((nrow, DA_V_DIM), F32)],
    )
    return pl.pallas_call(
        functools.partial(_dec_attn_kernel, pp=pp, n_new=n_new, lam_init=lam_init),
        grid_spec=grid_spec,
        out_shape=jax.ShapeDtypeStruct((Bd, nrow, DA_V_DIM), F32),
        compiler_params=_cp("parallel", "arbitrary"),
        name="diff_attn_decode",
    )(pt, lam, qrows, k_new, v_new_h, g, *([cache_kt] * pp), *([cache_v4] * pp))


def _rw_prep_kernel(h_ref, sh0_ref, mu_ref, w0_ref, a0_ref, wup_ref, aup_ref, gup_ref,
                    r_ref, k_ref, v_ref, lw_ref, a_ref, g_ref, prev_scr):
    cp = h_ref.shape[1]

    @pl.when(pl.program_id(1) == 0)
    def _():
        prev_scr[...] = sh0_ref[0]

    rw = h_ref[0]
    rowi = lax.broadcasted_iota(jnp.int32, rw.shape, 0)
    prev = jnp.where(rowi == 0, prev_scr[...], pltpu.roll(rw, 1, 0))
    prev_scr[...] = rw[cp - 1:cp, :]
    xm = rw + (prev - rw) * mu_ref[...]
    c1, c2, c3 = RW_WIDTH, 2 * RW_WIDTH, 3 * RW_WIDTH
    r_ref[0] = xm[:, :c1]
    k_ref[0] = xm[:, c1:c2]
    v_ref[0] = xm[:, c2:c3]
    lo = xm[:, c3:c3 + LANES]
    dg = xm[:, c3 + LANES:]
    z = -(w0_ref[...] + _mm3(_split(jnp.tanh(lo)), _split(wup_ref[...])))
    softplus = jnp.maximum(z, 0.0) + jnp.log1p(jnp.exp(-jnp.abs(z)))
    lw_ref[0] = -jnp.exp(-softplus - 0.5)
    a_ref[0] = jax.nn.sigmoid(a0_ref[...] + _mm3(_split(lo), _split(aup_ref[...])))
    g_ref[0] = _mm3(_split(jax.nn.sigmoid(dg)), _split(gup_ref[...]))


def _rw_prep(hrw3, sh0, mu, w0, a0, wup, aup, gup):
    B, T, _ = hrw3.shape
    cp = min(T, 256)
    full = lambda shape: pl.BlockSpec(shape, lambda b, c: (0,) * len(shape))
    out_blk = pl.BlockSpec((1, cp, RW_WIDTH), lambda b, c: (b, c, 0))
    out_sds = jax.ShapeDtypeStruct((B, T, RW_WIDTH), F32)
    return pl.pallas_call(
        _rw_prep_kernel,
        grid=(B, T // cp),
        in_specs=[pl.BlockSpec((1, cp, N_RW_PAD), lambda b, c: (b, c, 0)),
                  pl.BlockSpec((1, 1, N_RW_PAD), lambda b, c: (b, 0, 0)),
                  full((1, N_RW_PAD)), full((1, RW_WIDTH)), full((1, RW_WIDTH)),
                  full((LANES, RW_WIDTH)), full((LANES, RW_WIDTH)), full((G_PAD, RW_WIDTH))],
        out_specs=[out_blk] * 6,
        out_shape=[out_sds] * 6,
        scratch_shapes=[pltpu.VMEM((1, N_RW_PAD), F32)],
        compiler_params=_cp("parallel", "arbitrary"),
        name="rwkv_prep",
    )(hrw3, sh0, mu, w0, a0, wup, aup, gup)


def _split(x):
    hi = x.astype(BF16)
    lo = (x - hi.astype(F32)).astype(BF16)
    return hi, lo


def _mm3(a, b, dot=_dot):
    return dot(a[1], b[0]) + dot(a[0], b[1]) + dot(a[0], b[0])


def _mm_exact_rhs(x, m):
    hi, lo = _split(x)
    return _dot(lo, m) + _dot(hi, m)


def _rw_chunk(args, masks, t_valid):
    C = args[0][0].shape[0]
    incl_b, strict, incl, eye, seg_b, head_a, blockdiag, diag, levels = masks
    head_masks = (head_a, jnp.logical_not(head_a))
    P = range(len(args))
    PH = [(p, hd) for p in P for hd in range(2)]

    pre = []
    for (r, k, v, lw, a, g, kkp, kap, rkp, lnw, lnb, H) in args:
        if t_valid < C:
            valid = lax.broadcasted_iota(jnp.int32, (C, LANES), 0) < t_valid
            lw = jnp.where(valid, lw, 0.0)
            k = jnp.where(valid, k, 0.0)
            v = jnp.where(valid, v, 0.0)
        kk = k * kkp
        kk = kk * lax.rsqrt(jnp.maximum(_mm_exact_rhs(kk * kk, seg_b), 1e-24))
        k2 = k * (1.0 + (a - 1.0) * kap)
        b = kk * a
        lw_hi, lw_lo = _split(lw)
        lw_lo2 = (lw - lw_hi.astype(F32) - lw_lo.astype(F32)).astype(BF16)
        gam = _dot(incl_b, lw_lo2) + _dot(incl_b, lw_lo) + _dot(incl_b, lw_hi)
        gl = gam[C - 1:C, :]
        rt = r * jnp.exp(gam)
        at = -kk * jnp.exp(gam - lw)
        ginv = jnp.exp(-gam)
        gh = jnp.exp(gl - gam)
        pre.append(dict(r=r, v=v, k2=k2, g=g, rkp=rkp, lnw=lnw, lnb=lnb, H=H, gl=gl, rt=rt, at=at,
                        bt=_split(b * ginv), kt=_split(k2 * ginv), bh=b * gh, kh=k2 * gh,
                        Hs=_split(H), vs=_split(v)))

    stacked = C % LANES == 0
    cat0 = lambda xs: jnp.concatenate(xs, axis=0)
    if stacked:
        ar = [cat0([pre[p]['at'], pre[p]['rt']]) for p in P]
        bk = [tuple(cat0([pre[p]['bt'][i], pre[p]['kt'][i]]) for i in range(2)) for p in P]
        head2 = [cat0([hm, hm]) for hm in head_masks]
        A = {(p, hd): _mm3(_split(jnp.where(head2[hd], ar[p], 0.0)), bk[p], _dot_nt) for p, hd in PH}
        aab = {c: jnp.where(strict, A[c][:C, :C], 0.0) for c in PH}
        aak = {c: jnp.where(strict, A[c][:C, C:], 0.0) for c in PH}
        arb = {c: jnp.where(incl, A[c][C:, :C], 0.0) for c in PH}
        ark = {c: jnp.where(incl, A[c][C:, C:], 0.0) for c in PH}
        XR = [_mm3(_split(ar[p]), pre[p]['Hs']) for p in P]
        X = [xr[:C] for xr in XR]
        RH = [xr[C:] for xr in XR]
    else:
        atm = {(p, hd): _split(jnp.where(head_masks[hd], pre[p]['at'], 0.0)) for p, hd in PH}
        rtm = {(p, hd): _split(jnp.where(head_masks[hd], pre[p]['rt'], 0.0)) for p, hd in PH}
        aab = {c: jnp.where(strict, _mm3(atm[c], pre[c[0]]['bt'], _dot_nt), 0.0) for c in PH}
        aak = {c: jnp.where(strict, _mm3(atm[c], pre[c[0]]['kt'], _dot_nt), 0.0) for c in PH}
        arb = {c: jnp.where(incl, _mm3(rtm[c], pre[c[0]]['bt'], _dot_nt), 0.0) for c in PH}
        ark = {c: jnp.where(incl, _mm3(rtm[c], pre[c[0]]['kt'], _dot_nt), 0.0) for c in PH}
        X = [_mm3(_split(pre[p]['at']), pre[p]['Hs']) for p in P]
        RH = [_mm3(_split(pre[p]['rt']), pre[p]['Hs']) for p in P]
    tinv = {c: eye + jnp.where(levels[0], aab[c], 0.0) for c in PH}
    for lvl in levels[1:]:
        ts = {c: _split(tinv[c]) for c in PH}
        mid = {c: _split(_mm3(_split(jnp.where(lvl, aab[c], 0.0)), ts[c])) for c in PH}
        tinv = {c: tinv[c] + _mm3(ts[c], mid[c]) for c in PH}
    if stacked:
        ys2 = [_mm3(_split(cat0([aak[(p, 0)], aak[(p, 1)]])), pre[p]['vs']) for p in P]
        ysum = [_split(X[p] + jnp.where(head_a, ys2[p][:C], ys2[p][C:])) for p in P]
        u2 = [_mm3(_split(cat0([tinv[(p, 0)], tinv[(p, 1)]])), ysum[p]) for p in P]
        U = [jnp.where(head_a, u2[p][:C], u2[p][C:]) for p in P]
    else:
        ys = {c: _mm3(_split(aak[c]), pre[c[0]]['vs']) for c in PH}
        ysum = [_split(X[p] + jnp.where(head_a, ys[(p, 0)], ys[(p, 1)])) for p in P]
        u = {c: _mm3(_split(tinv[c]), ysum[c[0]]) for c in PH}
        U = [jnp.where(head_a, u[(p, 0)], u[(p, 1)]) for p in P]

    hn, uvh = [], []
    for p in P:
        dg = jnp.where(diag, jnp.exp(pre[p]['gl']), 0.0)
        lhs = _split(cat0([pre[p]['bh'], pre[p]['kh'], dg]))
        uvh.append(_split(cat0([U[p], pre[p]['v'], pre[p]['H']])))
        hn.append(jnp.where(blockdiag, _mm3(lhs, uvh[p], _dot_tn), 0.0))

    if stacked:
        both = lambda p, hd: jnp.concatenate([arb[(p, hd)], ark[(p, hd)]], axis=1)
        dd2 = [_mm3(_split(cat0([both(p, 0), both(p, 1)])), tuple(t[:2 * C] for t in uvh[p])) for p in P]
        dsel = [jnp.where(head_a, dd2[p][:C], dd2[p][C:]) for p in P]
    else:
        Us = [_split(U[p]) for p in P]
        dd = {c: _mm3(_split(arb[c]), Us[c[0]]) + _mm3(_split(ark[c]), pre[c[0]]['vs']) for c in PH}
        dsel = [jnp.where(head_a, dd[(p, 0)], dd[(p, 1)]) for p in P]
    inv_n = 1.0 / RW_HEAD
    outs = []
    for p in P:
        q = pre[p]
        O = RH[p] + dsel[p]
        mean = _mm_exact_rhs(O, seg_b) * inv_n
        d = O - mean
        var = _mm_exact_rhs(d * d, seg_b) * inv_n
        on = d * lax.rsqrt(var + LNX_EPS) * q['lnw'] + q['lnb']
        bonus = _mm_exact_rhs(q['r'] * q['k2'] * q['rkp'], seg_b) * q['v']
        outs.append(((on + bonus) * q['g'], hn[p]))
    return outs


def _rw_scan_kernel(r_ref, k_ref, v_ref, lw_ref, a_ref, g_ref, kk_ref, ka_ref, rk_ref,
                    lnw_ref, lnb_ref, h0_ref, o_ref, hout_ref, h_scr, *, t_valid):
    C = r_ref.shape[1]
    n_pairs = r_ref.shape[2] // LANES
    ci = pl.program_id(2)

    @pl.when(ci == 0)
    def _():
        h_scr[...] = h0_ref[0]

    ti = lax.broadcasted_iota(jnp.int32, (C, C), 0)
    si = lax.broadcasted_iota(jnp.int32, (C, C), 1)
    hr = lax.broadcasted_iota(jnp.int32, (LANES, LANES), 0)
    hc = lax.broadcasted_iota(jnp.int32, (LANES, LANES), 1)
    blockdiag = (hr // RW_HEAD) == (hc // RW_HEAD)
    levels = []
    m = 1
    while m < C:
        levels.append(((ti // (2 * m)) == (si // (2 * m))) & ((ti % (2 * m)) >= m) & ((si % (2 * m)) < m))
        m *= 2
    masks = ((si <= ti).astype(BF16), si < ti, si <= ti, (si == ti).astype(F32),
             blockdiag.astype(BF16),
             lax.broadcasted_iota(jnp.int32, (C, LANES), 1) < RW_HEAD, blockdiag, hr == hc, levels)
    sls = [slice(LANES * pi, LANES * (pi + 1)) for pi in range(n_pairs)]
    args = [(r_ref[0, :, sl], k_ref[0, :, sl], v_ref[0, :, sl], lw_ref[0, :, sl], a_ref[0, :, sl],
             g_ref[0, :, sl], kk_ref[:, sl], ka_ref[:, sl], rk_ref[:, sl], lnw_ref[:, sl], lnb_ref[:, sl],
             h_scr[pi]) for pi, sl in enumerate(sls)]
    res = _rw_chunk(args, masks, t_valid)
    for pi, sl in enumerate(sls):
        o_ref[0, :, sl] = res[pi][0].astype(o_ref.dtype)
        h_scr[pi] = res[pi][1]

    @pl.when(ci == pl.num_programs(2) - 1)
    def _():
        hout_ref[0] = h_scr[...]


def _rw_scan(r, k, v, lw, a, g, kk, ka, rk, lnw, lnb, h0, t_valid, out_dtype):
    B, T, _ = r.shape
    C = min(T, 128)
    pb = N_RW_PAIRS
    w = LANES * pb
    tok = pl.BlockSpec((1, C, w), lambda b, p, c: (b, c, p))
    vec = pl.BlockSpec((1, w), lambda b, p, c: (0, p))
    st = pl.BlockSpec((1, pb, LANES, LANES), lambda b, p, c: (b, p, 0, 0))
    return pl.pallas_call(
        functools.partial(_rw_scan_kernel, t_valid=t_valid),
        grid=(B, N_RW_PAIRS // pb, T // C),
        in_specs=[tok] * 6 + [vec] * 5 + [st],
        out_specs=[tok, st],
        out_shape=[jax.ShapeDtypeStruct((B, T, RW_WIDTH), out_dtype),
                   jax.ShapeDtypeStruct((B, N_RW_PAIRS, LANES, LANES), F32)],
        scratch_shapes=[pltpu.VMEM((pb, LANES, LANES), F32)],
        compiler_params=_cp("parallel", "parallel", "arbitrary"),
        name="rwkv_scan",
    )(r, k, v, lw, a, g, kk, ka, rk, lnw, lnb, h0)


def _out_proj_kernel(a1_ref, a2_ref, w_ref, res_ref, o_ref):
    acc = _dot(a1_ref[...].astype(BF16), w_ref[:DA_WIDTH, :])
    acc = acc + _dot(a2_ref[...].astype(BF16), w_ref[DA_WIDTH:, :])
    o_ref[...] = res_ref[...] + acc


def _out_proj(a1, a2, w, res):
    M, D = res.shape
    tm = min(M, 1024)
    tn = 512
    return pl.pallas_call(
        _out_proj_kernel,
        grid=(M // tm, D // tn),
        in_specs=[pl.BlockSpec((tm, DA_WIDTH), lambda i, j: (i, 0)),
                  pl.BlockSpec((tm, RW_WIDTH), lambda i, j: (i, 0)),
                  pl.BlockSpec((DA_WIDTH + RW_WIDTH, tn), lambda i, j: (0, j)),
                  pl.BlockSpec((tm, tn), lambda i, j: (i, j))],
        out_specs=pl.BlockSpec((tm, tn), lambda i, j: (i, j)),
        out_shape=jax.ShapeDtypeStruct((M, D), F32),
        compiler_params=_cp("parallel", "parallel"),
        name="out_proj",
    )(a1, a2, w, res)


def _ffn_up_kernel(x_ref, g_ref, wg_ref, wu_ref, o_ref, xn_ref):
    @pl.when(pl.program_id(1) == 0)
    def _():
        x = x_ref[...]
        ms = jnp.mean(x * x, axis=-1, keepdims=True)
        xn_ref[...] = (x * lax.rsqrt(ms + NORM_EPS) * g_ref[...]).astype(BF16)

    xn = xn_ref[...]
    gate = _dot(xn, wg_ref[...])
    up = _dot(xn, wu_ref[...])
    o_ref[...] = (gate * jax.nn.sigmoid(gate) * up).astype(o_ref.dtype)


def _ffn_up(x, g, wg, wu):
    M, D = x.shape
    tm = min(M, 1024)
    tn = 512
    return pl.pallas_call(
        _ffn_up_kernel,
        grid=(M // tm, D_FF // tn),
        in_specs=[pl.BlockSpec((tm, D), lambda i, j: (i, 0)),
                  pl.BlockSpec((1, D), lambda i, j: (0, 0)),
                  pl.BlockSpec((D, tn), lambda i, j: (0, j)),
                  pl.BlockSpec((D, tn), lambda i, j: (0, j))],
        out_specs=pl.BlockSpec((tm, tn), lambda i, j: (i, j)),
        out_shape=jax.ShapeDtypeStruct((M, D_FF), BF16),
        scratch_shapes=[pltpu.VMEM((tm, D), BF16)],
        compiler_params=_cp("parallel", "arbitrary"),
        name="ffn_gate_up",
    )(x, g, wg, wu)


def _ffn_down_kernel(a_ref, w_ref, res_ref, o_ref):
    o_ref[...] = res_ref[...] + _dot(a_ref[...], w_ref[...])


def _ffn_down(a, w, res):
    M, D = res.shape
    tm = min(M, 1024)
    tn = 512
    return pl.pallas_call(
        _ffn_down_kernel,
        grid=(M // tm, D // tn),
        in_specs=[pl.BlockSpec((tm, D_FF), lambda i, j: (i, 0)),
                  pl.BlockSpec((D_FF, tn), lambda i, j: (0, j)),
                  pl.BlockSpec((tm, tn), lambda i, j: (i, j))],
        out_specs=pl.BlockSpec((tm, tn), lambda i, j: (i, j)),
        out_shape=jax.ShapeDtypeStruct((M, D), F32),
        compiler_params=_cp("parallel", "parallel"),
        name="ffn_down",
    )(a, w, res)


def _rope_tables(pos):
    half = DA_QK_DIM // 2
    inv_freq = jnp.power(ROPE_THETA, -jnp.arange(half, dtype=F32) / half)
    ang = pos.astype(F32)[:, None] * inv_freq[None, :]
    cos = jnp.cos(ang)
    sin = jnp.sin(ang)
    reps = LANES // DA_QK_DIM
    cos_t = jnp.tile(jnp.concatenate([cos, cos], axis=-1), (1, reps))
    sin_t = jnp.tile(jnp.concatenate([-sin, sin], axis=-1), (1, reps))
    return cos_t, sin_t


def _state_to_pairs(S):
    B = S.shape[0]
    St = jnp.swapaxes(S, -1, -2).reshape(B, N_RW_PAIRS, 2, RW_HEAD, RW_HEAD)
    eye2 = jnp.eye(2, dtype=S.dtype)
    Hp = St[:, :, :, :, None, :] * eye2[None, None, :, None, :, None]
    return Hp.reshape(B, N_RW_PAIRS, LANES, LANES)


def _pairs_to_state(Hp):
    B = Hp.shape[0]
    H5 = Hp.reshape(B, N_RW_PAIRS, 2, RW_HEAD, 2, RW_HEAD)
    diag = jnp.stack([H5[:, :, 0, :, 0, :], H5[:, :, 1, :, 1, :]], axis=2)
    return jnp.swapaxes(diag.reshape(B, N_RW_HEADS, RW_HEAD, RW_HEAD), -1, -2)


def _decode_query_rows(qb):
    Bd = qb.shape[0]
    sub = jnp.arange(DA_WIDTH) // DA_QK_DIM
    want = 2 * jnp.arange(N_DA_HEADS)[:, None] + jnp.arange(2)[None, :]
    mask = sub[None, None, :] == want[:, :, None]
    rows = jnp.where(mask[None, :, :, None, :], qb[:, None, None, :, :], jnp.zeros((), qb.dtype))
    return rows.reshape(Bd, -1, DA_WIDTH)


def _decode_head_rows(o, n_new):
    Bd = o.shape[0]
    o5 = o.reshape(Bd, N_DA_HEADS, 2, n_new, DA_V_DIM)[:, :, 0]
    return jnp.transpose(o5, (0, 2, 1, 3)).reshape(Bd, n_new, DA_WIDTH)


def _pad_rows(x, n):
    return jnp.pad(x, ((0, 0), (0, n - x.shape[1]), (0, 0)))


def _layer(x3, cos, sin, S0, shift0, lam, lam_init, p, decode=None):
    B, T, D = x3.shape
    M = B * T
    x = x3.reshape(M, D)
    hda = _norm_mm(x, p['norm_mix'], p['w_da'], 1024)
    hrw = _norm_mm(x, p['norm_mix'], p['w_rw'], 1152)
    qb, kf, kb, vb = _qk_post(hda, p['q_gain'], p['k_gain'], cos, sin, T)
    hda3 = hda.reshape(B, T, 3 * DA_WIDTH)
    v_new = hda3[:, :, 2 * DA_WIDTH:]
    if decode is None:
        o_da = _attn_prompt(lam, qb.reshape(B, T, DA_WIDTH), kb.reshape(B, T, DA_WIDTH),
                            vb.reshape(B, T, DA_WIDTH), p['subln_gain'], lam_init)
    else:
        page_table, cache_kt, cache_v4, layer = decode
        v_new_h = jnp.transpose(v_new.reshape(B, T, N_DA_HEADS, DA_V_DIM), (0, 2, 1, 3))
        v_new_h = jnp.pad(v_new_h, ((0, 0), (0, 0), (0, NEW_PAD - T), (0, 0)))
        o_da = _attn_sample(page_table, lam, _decode_query_rows(qb.reshape(B, T, DA_WIDTH)),
                            _pad_rows(kf.reshape(B, T, DA_WIDTH), NEW_PAD), v_new_h, p['subln_gain'],
                            cache_kt, cache_v4, layer, lam_init)
        o_da = _decode_head_rows(o_da, T)
    hrw3 = hrw.reshape(B, T, N_RW_PAD)
    shift_new = hrw3[:, -1, :N_RW_COLS]
    t_pad = max(T, 8)
    hrw3p = _pad_rows(hrw3, t_pad)
    sh0 = jnp.pad(shift0, ((0, 0), (0, N_RW_PAD - N_RW_COLS)))[:, None, :]
    r, k, v, lw, a, g = _rw_prep(hrw3p, sh0, p['mu_shift'], p['w0'], p['a0'], p['w_up'], p['a_up'],
                                 p['g_up'])
    o_rw, hp = _rw_scan(r, k, v, lw, a, g, p['k_k'], p['k_a'], p['r_k'], p['lnx_w'], p['lnx_b'],
                        _state_to_pairs(S0), T, BF16 if decode is None else F32)
    o_rw = o_rw[:, :T]
    S_new = _pairs_to_state(hp)
    x1 = _out_proj(o_da.reshape(M, DA_WIDTH), o_rw.reshape(M, RW_WIDTH), p['w_out'], x)
    ff = _ffn_up(x1, p['norm_ffn'], p['w_gate'], p['w_up_ffn'])
    x2 = _ffn_down(ff, p['w_down'], x1)
    return x2.reshape(B, T, D), kf, v_new, S_new, shift_new


def kernel(x_prompt, x_sample, cache_k, cache_v, state_wkv, state_shift, page_table, norm_mix, w_in, q_gain, k_gain, lambda_q1, lambda_k1, lambda_q2, lambda_k2, subln_gain, mu_shift, w0, w_up, a0, a_up, g_up, k_k, k_a, r_k, lnx_w, lnx_b, w_out, norm_ffn, w_gate, w_up_ffn, w_down):
    B, S, _ = x_prompt.shape
    Bd, T, _ = x_sample.shape
    depth = w_in.shape[0]
    n_pages = page_table.shape[1]
    past_len = n_pages * PAGE_SIZE
    cos_p, sin_p = _rope_tables(jnp.arange(S, dtype=jnp.int32))
    cos_s, sin_s = _rope_tables(past_len + jnp.arange(T, dtype=jnp.int32))
    cos_s = jnp.tile(cos_s, (Bd, 1))
    sin_s = jnp.tile(sin_s, (Bd, 1))
    n_pool = cache_k.shape[1]
    cache_kt = jnp.transpose(cache_k, (0, 1, 3, 4, 2)).reshape(depth, n_pool, DA_WIDTH, PAGE_SIZE)
    cache_v4 = cache_v.reshape(depth, n_pool, PAGE_SIZE * N_DA_HEADS, DA_V_DIM)
    row = lambda t: t.reshape(1, -1)
    tile_gain = lambda t: jnp.tile(t, LANES // DA_QK_DIM).reshape(1, LANES)

    xp, xs = x_prompt, x_sample
    outs = [[] for _ in range(8)]
    for l in range(depth):
        lam_init = 0.8 - 0.6 * math.exp(-0.3 * l)
        lam = (jnp.exp(jnp.sum(lambda_q1[l] * lambda_k1[l])) - jnp.exp(jnp.sum(lambda_q2[l] * lambda_k2[l]))
               + lam_init).reshape(1).astype(F32)
        w_in_b = w_in[l].astype(BF16)
        p = {
            'norm_mix': row(norm_mix[l]),
            'w_da': w_in_b[:, :3 * DA_WIDTH],
            'w_rw': jnp.pad(w_in_b[:, 3 * DA_WIDTH:], ((0, 0), (0, N_RW_PAD - N_RW_COLS))),
            'q_gain': tile_gain(q_gain[l]), 'k_gain': tile_gain(k_gain[l]),
            'subln_gain': row(subln_gain[l]),
            'mu_shift': jnp.pad(row(mu_shift[l]), ((0, 0), (0, N_RW_PAD - N_RW_COLS))),
            'w0': row(w0[l]), 'a0': row(a0[l]),
            'w_up': jnp.pad(w_up[l], ((0, LANES - W_LORA), (0, 0))),
            'a_up': jnp.pad(a_up[l], ((W_LORA, LANES - W_LORA - A_LORA), (0, 0))),
            'g_up': jnp.pad(g_up[l], ((0, G_PAD - G_LORA), (0, 0))),
            'k_k': row(k_k[l]), 'k_a': row(k_a[l]), 'r_k': row(r_k[l]),
            'lnx_w': row(lnx_w[l]), 'lnx_b': row(lnx_b[l]),
            'w_out': w_out[l].astype(BF16), 'norm_ffn': row(norm_ffn[l]),
            'w_gate': w_gate[l].astype(BF16), 'w_up_ffn': w_up_ffn[l].astype(BF16),
            'w_down': w_down[l].astype(BF16),
        }
        S0_p = jnp.zeros((B, N_RW_HEADS, RW_HEAD, RW_HEAD), F32)
        sh0_p = jnp.zeros((B, N_RW_COLS), F32)
        xp, k_new, v_new, S_new, sh_new = _layer(xp, cos_p, sin_p, S0_p, sh0_p, lam, lam_init, p)
        outs[0].append(k_new.reshape(B, S // PAGE_SIZE, PAGE_SIZE, N_DA_SUB, DA_QK_DIM))
        outs[1].append(v_new.reshape(B, S // PAGE_SIZE, PAGE_SIZE, N_DA_HEADS, DA_V_DIM))
        outs[2].append(S_new)
        outs[3].append(sh_new)
        xs, k_new, v_new, S_new, sh_new = _layer(xs, cos_s, sin_s, state_wkv[l], state_shift[l], lam,
                                                 lam_init, p, decode=(page_table, cache_kt, cache_v4, l))
        outs[4].append(k_new.reshape(Bd, T, N_DA_SUB, DA_QK_DIM))
        outs[5].append(v_new.reshape(Bd, T, N_DA_HEADS, DA_V_DIM))
        outs[6].append(S_new)
        outs[7].append(sh_new)
    return (xp, xs) + tuple(jnp.stack(o) for o in outs)
```

```python
import functools
import math

import jax
import jax.numpy as jnp
from jax import lax
from jax.experimental import pallas as pl
from jax.experimental.pallas import tpu as pltpu

F32 = jnp.float32
BF16 = jnp.bfloat16

D_MODEL = 2048
PAGE_SIZE = 128
DA_WIDTH = 1024
RW_WIDTH = 1024
DA_V_DIM = 128
N_DA_HEADS = 8
DA_QK_DIM = 64
N_DA_SUB = 16
RW_HEAD = 64
N_RW_HEADS = 16
N_RW_PAIRS = N_RW_HEADS // 2
W_LORA = 64
A_LORA = 64
G_LORA = 160
N_RW_COLS = 3 * RW_WIDTH + W_LORA + A_LORA + G_LORA
LANES = 128
N_RW_PAD = ((N_RW_COLS + LANES - 1) // LANES) * LANES
G_PAD = N_RW_PAD - 3 * RW_WIDTH - LANES
D_FF = 5632
ROPE_THETA = 10000.0
NORM_EPS = 1e-6
LNX_EPS = 64e-5
MASK_VALUE = -1e30
DA_SCALE = DA_QK_DIM ** -0.5
VMEM_LIMIT_BYTES = 56 * 1024 * 1024
NEW_PAD = 16


def _cp(*sem):
    return pltpu.CompilerParams(dimension_semantics=sem, vmem_limit_bytes=VMEM_LIMIT_BYTES)


def _dot(a, b, precision=None):
    return jnp.dot(a, b, preferred_element_type=F32, precision=precision)


def _dot_nt(a, b, precision=None):
    return lax.dot_general(a, b, (((1,), (1,)), ((), ())), preferred_element_type=F32,
                           precision=precision)


def _dot_tn(a, b, precision=None):
    return lax.dot_general(a, b, (((0,), (0,)), ((), ())), preferred_element_type=F32,
                           precision=precision)


def _seg_ones(n, seg):
    r = lax.broadcasted_iota(jnp.int32, (n, n), 0) // seg
    c = lax.broadcasted_iota(jnp.int32, (n, n), 1) // seg
    return (r == c).astype(F32)


def _norm_mm_kernel(x_ref, g_ref, w_ref, o_ref, xn_ref):
    @pl.when(pl.program_id(1) == 0)
    def _():
        x = x_ref[...]
        ms = jnp.mean(x * x, axis=-1, keepdims=True)
        xn_ref[...] = (x * lax.rsqrt(ms + NORM_EPS) * g_ref[...]).astype(BF16)

    o_ref[...] = _dot(xn_ref[...], w_ref[...])


def _norm_mm(x, g, w, tn):
    M, D = x.shape
    N = w.shape[1]
    tm = min(M, 1024)
    return pl.pallas_call(
        _norm_mm_kernel,
        grid=(M // tm, N // tn),
        in_specs=[pl.BlockSpec((tm, D), lambda i, j: (i, 0)),
                  pl.BlockSpec((1, D), lambda i, j: (0, 0)),
                  pl.BlockSpec((D, tn), lambda i, j: (0, j))],
        out_specs=pl.BlockSpec((tm, tn), lambda i, j: (i, j)),
        out_shape=jax.ShapeDtypeStruct((M, N), F32),
        scratch_shapes=[pltpu.VMEM((tm, D), BF16)],
        compiler_params=_cp("parallel", "arbitrary"),
        name="norm_in_proj",
    )(x, g, w)


def _qk_post_kernel(hq_ref, hk_ref, hv_ref, qg_ref, kg_ref, cos_ref, sin_ref, qb_ref, kf_ref, kb_ref, vb_ref,
                    vf_ref):
    tm = hq_ref.shape[0]
    vf_ref[...] = hv_ref[...]
    vb_ref[...] = hv_ref[...].astype(BF16)
    lane = lax.broadcasted_iota(jnp.int32, (tm, LANES), 1)
    first = (lane % DA_QK_DIM) < (DA_QK_DIM // 2)
    seg = (_seg_ones(LANES, DA_QK_DIM) * (1.0 / DA_QK_DIM)).astype(BF16)
    cos = cos_ref[...]
    sin = sin_ref[...]

    def proc(x, g):
        ms = _mm_exact_rhs(x * x, seg)
        y = x * lax.rsqrt(ms + NORM_EPS) * g
        partner = jnp.where(first, pltpu.roll(y, LANES - DA_QK_DIM // 2, 1),
                            pltpu.roll(y, DA_QK_DIM // 2, 1))
        return y * cos + partner * sin

    for c in range(DA_WIDTH // LANES):
        sl = slice(LANES * c, LANES * (c + 1))
        q = proc(hq_ref[:, sl], qg_ref[...])
        qb_ref[:, sl] = (q * DA_SCALE).astype(BF16)
        k = proc(hk_ref[:, sl], kg_ref[...])
        kf_ref[:, sl] = k
        kb_ref[:, sl] = k.astype(BF16)


def _qk_post(hda, qg, kg, cos, sin, T):
    M = hda.shape[0]
    tm = min(M, 512)
    if cos.shape[0] == M:
        tab_map = lambda i: (i, 0)
    else:
        nper = T // tm
        tab_map = lambda i: (i % nper, 0)
    blk = lambda c: pl.BlockSpec((tm, DA_WIDTH), lambda i: (i, c))
    vec = pl.BlockSpec((1, LANES), lambda i: (0, 0))
    tab = pl.BlockSpec((tm, LANES), tab_map)
    return pl.pallas_call(
        _qk_post_kernel,
        grid=(M // tm,),
        in_specs=[blk(0), blk(1), blk(2), vec, vec, tab, tab],
        out_specs=[blk(0), blk(0), blk(0), blk(0), blk(0)],
        out_shape=[jax.ShapeDtypeStruct((M, DA_WIDTH), BF16),
                   jax.ShapeDtypeStruct((M, DA_WIDTH), F32),
                   jax.ShapeDtypeStruct((M, DA_WIDTH), BF16),
                   jax.ShapeDtypeStruct((M, DA_WIDTH), BF16),
                   jax.ShapeDtypeStruct((M, DA_WIDTH), F32)],
        compiler_params=_cp("parallel"),
        name="qk_norm_rope",
    )(hda, hda, hda, qg, kg, cos, sin)


def _subln(o, g, lam_init):
    ms = jnp.mean(o * o, axis=-1, keepdims=True)
    return o * lax.rsqrt(ms + NORM_EPS) * g * (1.0 - lam_init)


def _attn_kernel(lam_ref, q_ref, k_ref, v_ref, g_ref, o_ref, m_scr, l_scr, acc_scr, *, tq, tk, lam_init):
    qi = pl.program_id(2)
    n_heads = q_ref.shape[2] // LANES
    lane = lax.broadcasted_iota(jnp.int32, (tq, LANES), 1)
    chains = [(hd, mi) for hd in range(n_heads) for mi in range(2)]
    qs = []
    for hd, mi in chains:
        q = q_ref[0, :, LANES * hd:LANES * (hd + 1)]
        keep = (lane < DA_QK_DIM) if mi == 0 else (lane >= DA_QK_DIM)
        qs.append(jnp.where(keep, q, jnp.zeros_like(q)))
    m_scr[...] = jnp.full(m_scr.shape, MASK_VALUE, F32)
    l_scr[...] = jnp.zeros(l_scr.shape, F32)
    acc_scr[...] = jnp.zeros(acc_scr.shape, F32)

    def block(j, masked):
        off = pl.multiple_of(j * tk, tk)
        kbs = [k_ref[0, pl.ds(off, tk), LANES * hd:LANES * (hd + 1)] for hd in range(n_heads)]
        vbs = [v_ref[0, pl.ds(off, tk), LANES * hd:LANES * (hd + 1)] for hd in range(n_heads)]
        nc = range(len(chains))
        ms = [m_scr[c] for c in nc]
        ls = [l_scr[c] for c in nc]
        accs = [acc_scr[c] for c in nc]
        ss = [_dot_nt(qs[c], kbs[chains[c][0]]) for c in nc]
        if masked:
            row = qi * tq + lax.broadcasted_iota(jnp.int32, (tq, tk), 0)
            col = off + lax.broadcasted_iota(jnp.int32, (tq, tk), 1)
            ss = [jnp.where(col <= row, s, MASK_VALUE) for s in ss]
        mns = [jnp.maximum(ms[c], jnp.max(ss[c], axis=-1, keepdims=True)) for c in nc]
        als = [jnp.exp(ms[c] - mns[c]) for c in nc]
        ps = [jnp.exp(ss[c] - jnp.concatenate([mns[c]] * (tk // LANES), axis=1)) for c in nc]
        sums = [functools.reduce(lambda a, b: a + b, [ps[c][:, LANES * i:LANES * (i + 1)]
                                                      for i in range(tk // LANES)]) for c in nc]
        pvs = [_dot(ps[c].astype(BF16), vbs[chains[c][0]]) for c in nc]
        for c in nc:
            m_scr[c] = mns[c]
            l_scr[c] = ls[c] * als[c] + sums[c]
            acc_scr[c] = accs[c] * als[c] + pvs[c]

    n_full = (qi * tq) // tk

    def body(j, c):
        block(j, False)
        return c

    lax.fori_loop(0, n_full, body, 0)
    block(n_full, True)
    for hd in range(n_heads):
        c1, c2 = 2 * hd, 2 * hd + 1
        l1 = jnp.sum(l_scr[c1], axis=-1, keepdims=True)
        l2 = jnp.sum(l_scr[c2], axis=-1, keepdims=True)
        o = acc_scr[c1] / l1 - lam_ref[0] * (acc_scr[c2] / l2)
        o_ref[0, :, LANES * hd:LANES * (hd + 1)] = _subln(o, g_ref[...], lam_init).astype(o_ref.dtype)


def _attn_prompt(lam, qb, kb, vb, g, lam_init):
    B, S, _ = qb.shape
    tq = min(S, 128)
    tk = min(S, 256)
    hb = 2
    w = LANES * hb
    return pl.pallas_call(
        functools.partial(_attn_kernel, tq=tq, tk=tk, lam_init=lam_init),
        grid=(B, N_DA_HEADS // hb, S // tq),
        in_specs=[pl.BlockSpec(memory_space=pltpu.SMEM),
                  pl.BlockSpec((1, tq, w), lambda b, h, i: (b, i, h)),
                  pl.BlockSpec((1, S, w), lambda b, h, i: (b, 0, h)),
                  pl.BlockSpec((1, S, w), lambda b, h, i: (b, 0, h)),
                  pl.BlockSpec((1, DA_V_DIM), lambda b, h, i: (0, 0))],
        out_specs=pl.BlockSpec((1, tq, w), lambda b, h, i: (b, i, h)),
        out_shape=jax.ShapeDtypeStruct((B, S, DA_WIDTH), BF16),
        scratch_shapes=[pltpu.VMEM((2 * hb, tq, LANES), F32), pltpu.VMEM((2 * hb, tq, LANES), F32),
                        pltpu.VMEM((2 * hb, tq, DA_V_DIM), F32)],
        compiler_params=_cp("parallel", "parallel", "parallel"),
        name="diff_attn_prompt",
    )(lam, qb, kb, vb, g)


def _dec_attn_kernel(pt_ref, lam_ref, q_ref, kn_ref, vn_ref, g_ref, *rest, pp, n_new, lam_init):
    k_refs = rest[:pp]
    v_refs = rest[pp:2 * pp]
    o_ref = rest[2 * pp]
    m_scr, l_scr, acc_scr = rest[2 * pp + 1:]
    p_id = pl.program_id(1)
    rows_h = 2 * n_new

    @pl.when(p_id == 0)
    def _():
        m_scr[...] = jnp.full(m_scr.shape, MASK_VALUE, F32)
        l_scr[...] = jnp.zeros(l_scr.shape, F32)
        acc_scr[...] = jnp.zeros(acc_scr.shape, F32)

    q = q_ref[0]

    def update(s, v_of_head):
        m = m_scr[...]
        mn = jnp.maximum(m, jnp.max(s, axis=-1, keepdims=True))
        al = jnp.exp(m - mn)
        p = jnp.exp(s - mn)
        l_scr[...] = l_scr[...] * al + jnp.sum(p, axis=-1, keepdims=True)
        pv = [_dot(p[rows_h * h:rows_h * (h + 1)].astype(BF16), v_of_head(h)) for h in range(N_DA_HEADS)]
        acc_scr[...] = acc_scr[...] * al + jnp.concatenate(pv, axis=0)
        m_scr[...] = mn

    def page_values(h):
        return jnp.concatenate([v_refs[i].at[0, 0][pl.ds(h, PAGE_SIZE, stride=N_DA_HEADS), :].astype(BF16)
                                for i in range(pp)], axis=0)

    s = jnp.concatenate([_dot(q, k_refs[i][0, 0].astype(BF16)) for i in range(pp)], axis=1)
    update(s, page_values)

    @pl.when(p_id == pl.num_programs(1) - 1)
    def _():
        s = _dot_nt(q, kn_ref[0].astype(BF16))
        r = lax.broadcasted_iota(jnp.int32, s.shape, 0)
        t = lax.broadcasted_iota(jnp.int32, s.shape, 1)
        s = jnp.where(t <= (r % n_new), s, MASK_VALUE)
        update(s, lambda h: vn_ref[0, h].astype(BF16))
        n = acc_scr[...] / l_scr[...]
        comb = n - lam_ref[0] * pltpu.roll(n, n.shape[0] - n_new, 0)
        o_ref[0] = _subln(comb, g_ref[...], lam_init)


def _attn_sample(page_table, lam, qrows, k_new, v_new_h, g, cache_kt, cache_v4, layer, lam_init, pp=8):
    Bd, nrow, _ = qrows.shape
    n_new = nrow // (2 * N_DA_HEADS)
    n_pages = page_table.shape[1]
    pt = page_table.reshape(-1)

    def page_spec(i):
        return pl.BlockSpec((1, 1, DA_WIDTH, LANES),
                            lambda b, p, pt_ref: (layer, pt_ref[b * n_pages + p * pp + i], 0, 0))

    per_b = lambda shape: pl.BlockSpec((1,) + shape, lambda b, p, pt_ref: (b,) + (0,) * len(shape))
    grid_spec = pltpu.PrefetchScalarGridSpec(
        num_scalar_prefetch=1,
        grid=(Bd, n_pages // pp),
        in_specs=[pl.BlockSpec(memory_space=pltpu.SMEM), per_b((nrow, DA_WIDTH)), per_b(k_new.shape[1:]),
                  per_b(v_new_h.shape[1:]), pl.BlockSpec((1, DA_V_DIM), lambda b, p, pt_ref: (0, 0))]
                 + [page_spec(i) for i in range(pp)] + [page_spec(i) for i in range(pp)],
        out_specs=per_b((nrow, DA_V_DIM)),
        scratch_shapes=[pltpu.VMEM((nrow, 1), F32), pltpu.VMEM((nrow, 1), F32),
                        pltpu.VMEM((nrow, DA_V_DIM), F32)],
    )
    return pl.pallas_call(
        functools.partial(_dec_attn_kernel, pp=pp, n_new=n_new, lam_init=lam_init),
        grid_spec=grid_spec,
        out_shape=jax.ShapeDtypeStruct((Bd, nrow, DA_V_DIM), F32),
        compiler_params=_cp("parallel", "arbitrary"),
        name="diff_attn_decode",
    )(pt, lam, qrows, k_new, v_new_h, g, *([cache_kt] * pp), *([cache_v4] * pp))


def _rw_prep_kernel(h_ref, sh0_ref, mu_ref, w0_ref, a0_ref, wup_ref, aup_ref, gup_ref,
                    r_ref, k_ref, v_ref, lw_ref, a_ref, g_ref, prev_scr):
    cp = h_ref.shape[1]

    @pl.when(pl.program_id(1) == 0)
    def _():
        prev_scr[...] = sh0_ref[0]

    rw = h_ref[0]
    rowi = lax.broadcasted_iota(jnp.int32, rw.shape, 0)
    prev = jnp.where(rowi == 0, prev_scr[...], pltpu.roll(rw, 1, 0))
    prev_scr[...] = rw[cp - 1:cp, :]
    xm = rw + (prev - rw) * mu_ref[...]
    c1, c2, c3 = RW_WIDTH, 2 * RW_WIDTH, 3 * RW_WIDTH
    r_ref[0] = xm[:, :c1]
    k_ref[0] = xm[:, c1:c2]
    v_ref[0] = xm[:, c2:c3]
    lo = xm[:, c3:c3 + LANES]
    dg = xm[:, c3 + LANES:]
    z = -(w0_ref[...] + _mm3(_split(jnp.tanh(lo)), _split(wup_ref[...])))
    softplus = jnp.maximum(z, 0.0) + jnp.log1p(jnp.exp(-jnp.abs(z)))
    lw_ref[0] = -jnp.exp(-softplus - 0.5)
    a_ref[0] = jax.nn.sigmoid(a0_ref[...] + _mm3(_split(lo), _split(aup_ref[...])))
    g_ref[0] = _mm3(_split(jax.nn.sigmoid(dg)), _split(gup_ref[...]))


def _rw_prep(hrw3, sh0, mu, w0, a0, wup, aup, gup):
    B, T, _ = hrw3.shape
    cp = min(T, 256)
    full = lambda shape: pl.BlockSpec(shape, lambda b, c: (0,) * len(shape))
    out_blk = pl.BlockSpec((1, cp, RW_WIDTH), lambda b, c: (b, c, 0))
    out_sds = jax.ShapeDtypeStruct((B, T, RW_WIDTH), F32)
    return pl.pallas_call(
        _rw_prep_kernel,
        grid=(B, T // cp),
        in_specs=[pl.BlockSpec((1, cp, N_RW_PAD), lambda b, c: (b, c, 0)),
                  pl.BlockSpec((1, 1, N_RW_PAD), lambda b, c: (b, 0, 0)),
                  full((1, N_RW_PAD)), full((1, RW_WIDTH)), full((1, RW_WIDTH)),
                  full((LANES, RW_WIDTH)), full((LANES, RW_WIDTH)), full((G_PAD, RW_WIDTH))],
        out_specs=[out_blk] * 6,
        out_shape=[out_sds] * 6,
        scratch_shapes=[pltpu.VMEM((1, N_RW_PAD), F32)],
        compiler_params=_cp("parallel", "arbitrary"),
        name="rwkv_prep",
    )(hrw3, sh0, mu, w0, a0, wup, aup, gup)


def _split(x):
    hi = x.astype(BF16)
    lo = (x - hi.astype(F32)).astype(BF16)
    return hi, lo


def _mm3(a, b, dot=_dot):
    return dot(a[1], b[0]) + dot(a[0], b[1]) + dot(a[0], b[0])


def _mm_exact_rhs(x, m):
    hi, lo = _split(x)
    return _dot(lo, m) + _dot(hi, m)


def _rw_chunk(args, masks, t_valid):
    C = args[0][0].shape[0]
    incl_b, strict, incl, eye, seg_b, head_a, blockdiag, diag, levels = masks
    head_masks = (head_a, jnp.logical_not(head_a))
    P = range(len(args))
    PH = [(p, hd) for p in P for hd in range(2)]

    pre = []
    for (r, k, v, lw, a, g, kkp, kap, rkp, lnw, lnb, H) in args:
        if t_valid < C:
            valid = lax.broadcasted_iota(jnp.int32, (C, LANES), 0) < t_valid
            lw = jnp.where(valid, lw, 0.0)
            k = jnp.where(valid, k, 0.0)
            v = jnp.where(valid, v, 0.0)
        kk = k * kkp
        kk = kk * lax.rsqrt(jnp.maximum(_mm_exact_rhs(kk * kk, seg_b), 1e-24))
        k2 = k * (1.0 + (a - 1.0) * kap)
        b = kk * a
        lw_hi, lw_lo = _split(lw)
        lw_lo2 = (lw - lw_hi.astype(F32) - lw_lo.astype(F32)).astype(BF16)
        gam = _dot(incl_b, lw_lo2) + _dot(incl_b, lw_lo) + _dot(incl_b, lw_hi)
        gl = gam[C - 1:C, :]
        rt = r * jnp.exp(gam)
        at = -kk * jnp.exp(gam - lw)
        ginv = jnp.exp(-gam)
        gh = jnp.exp(gl - gam)
        pre.append(dict(r=r, v=v, k2=k2, g=g, rkp=rkp, lnw=lnw, lnb=lnb, H=H, gl=gl, rt=rt, at=at,
                        bt=_split(b * ginv), kt=_split(k2 * ginv), bh=b * gh, kh=k2 * gh,
                        Hs=_split(H), vs=_split(v)))

    stacked = C % LANES == 0
    cat0 = lambda xs: jnp.concatenate(xs, axis=0)
    if stacked:
        ar = [cat0([pre[p]['at'], pre[p]['rt']]) for p in P]
        bk = [tuple(cat0([pre[p]['bt'][i], pre[p]['kt'][i]]) for i in range(2)) for p in P]
        head2 = [cat0([hm, hm]) for hm in head_masks]
        A = {(p, hd): _mm3(_split(jnp.where(head2[hd], ar[p], 0.0)), bk[p], _dot_nt) for p, hd in PH}
        aab = {c: jnp.where(strict, A[c][:C, :C], 0.0) for c in PH}
        aak = {c: jnp.where(strict, A[c][:C, C:], 0.0) for c in PH}
        arb = {c: jnp.where(incl, A[c][C:, :C], 0.0) for c in PH}
        ark = {c: jnp.where(incl, A[c][C:, C:], 0.0) for c in PH}
        XR = [_mm3(_split(ar[p]), pre[p]['Hs']) for p in P]
        X = [xr[:C] for xr in XR]
        RH = [xr[C:] for xr in XR]
    else:
        atm = {(p, hd): _split(jnp.where(head_masks[hd], pre[p]['at'], 0.0)) for p, hd in PH}
        rtm = {(p, hd): _split(jnp.where(head_masks[hd], pre[p]['rt'], 0.0)) for p, hd in PH}
        aab = {c: jnp.where(strict, _mm3(atm[c], pre[c[0]]['bt'], _dot_nt), 0.0) for c in PH}
        aak = {c: jnp.where(strict, _mm3(atm[c], pre[c[0]]['kt'], _dot_nt), 0.0) for c in PH}
        arb = {c: jnp.where(incl, _mm3(rtm[c], pre[c[0]]['bt'], _dot_nt), 0.0) for c in PH}
        ark = {c: jnp.where(incl, _mm3(rtm[c], pre[c[0]]['kt'], _dot_nt), 0.0) for c in PH}
        X = [_mm3(_split(pre[p]['at']), pre[p]['Hs']) for p in P]
        RH = [_mm3(_split(pre[p]['rt']), pre[p]['Hs']) for p in P]
    tinv = {c: eye + jnp.where(levels[0], aab[c], 0.0) for c in PH}
    for lvl in levels[1:]:
        ts = {c: _split(tinv[c]) for c in PH}
        mid = {c: _split(_mm3(_split(jnp.where(lvl, aab[c], 0.0)), ts[c])) for c in PH}
        tinv = {c: tinv[c] + _mm3(ts[c], mid[c]) for c in PH}
    if stacked:
        ys2 = [_mm3(_split(cat0([aak[(p, 0)], aak[(p, 1)]])), pre[p]['vs']) for p in P]
        ysum = [_split(X[p] + jnp.where(head_a, ys2[p][:C], ys2[p][C:])) for p in P]
        u2 = [_mm3(_split(cat0([tinv[(p, 0)], tinv[(p, 1)]])), ysum[p]) for p in P]
        U = [jnp.where(head_a, u2[p][:C], u2[p][C:]) for p in P]
    else:
        ys = {c: _mm3(_split(aak[c]), pre[c[0]]['vs']) for c in PH}
        ysum = [_split(X[p] + jnp.where(head_a, ys[(p, 0)], ys[(p, 1)])) for p in P]
        u = {c: _mm3(_split(tinv[c]), ysum[c[0]]) for c in PH}
        U = [jnp.where(head_a, u[(p, 0)], u[(p, 1)]) for p in P]

    hn, uvh = [], []
    for p in P:
        dg = jnp.where(diag, jnp.exp(pre[p]['gl']), 0.0)
        lhs = _split(cat0([pre[p]['bh'], pre[p]['kh'], dg]))
        uvh.append(_split(cat0([U[p], pre[p]['v'], pre[p]['H']])))
        hn.append(jnp.where(blockdiag, _mm3(lhs, uvh[p], _dot_tn), 0.0))

    if stacked:
        both = lambda p, hd: jnp.concatenate([arb[(p, hd)], ark[(p, hd)]], axis=1)
        dd2 = [_mm3(_split(cat0([both(p, 0), both(p, 1)])), tuple(t[:2 * C] for t in uvh[p])) for p in P]
        dsel = [jnp.where(head_a, dd2[p][:C], dd2[p][C:]) for p in P]
    else:
        Us = [_split(U[p]) for p in P]
        dd = {c: _mm3(_split(arb[c]), Us[c[0]]) + _mm3(_split(ark[c]), pre[c[0]]['vs']) for c in PH}
        dsel = [jnp.where(head_a, dd[(p, 0)], dd[(p, 1)]) for p in P]
    inv_n = 1.0 / RW_HEAD
    outs = []
    for p in P:
        q = pre[p]
        O = RH[p] + dsel[p]
        mean = _mm_exact_rhs(O, seg_b) * inv_n
        d = O - mean
        var = _mm_exact_rhs(d * d, seg_b) * inv_n
        on = d * lax.rsqrt(var + LNX_EPS) * q['lnw'] + q['lnb']
        bonus = _mm_exact_rhs(q['r'] * q['k2'] * q['rkp'], seg_b) * q['v']
        outs.append(((on + bonus) * q['g'], hn[p]))
    return outs


def _rw_scan_kernel(r_ref, k_ref, v_ref, lw_ref, a_ref, g_ref, kk_ref, ka_ref, rk_ref,
                    lnw_ref, lnb_ref, h0_ref, o_ref, hout_ref, h_scr, *, t_valid):
    C = r_ref.shape[1]
    n_pairs = r_ref.shape[2] // LANES
    ci = pl.program_id(2)

    @pl.when(ci == 0)
    def _():
        h_scr[...] = h0_ref[0]

    ti = lax.broadcasted_iota(jnp.int32, (C, C), 0)
    si = lax.broadcasted_iota(jnp.int32, (C, C), 1)
    hr = lax.broadcasted_iota(jnp.int32, (LANES, LANES), 0)
    hc = lax.broadcasted_iota(jnp.int32, (LANES, LANES), 1)
    blockdiag = (hr // RW_HEAD) == (hc // RW_HEAD)
    levels = []
    m = 1
    while m < C:
        levels.append(((ti // (2 * m)) == (si // (2 * m))) & ((ti % (2 * m)) >= m) & ((si % (2 * m)) < m))
        m *= 2
    masks = ((si <= ti).astype(BF16), si < ti, si <= ti, (si == ti).astype(F32),
             blockdiag.astype(BF16),
             lax.broadcasted_iota(jnp.int32, (C, LANES), 1) < RW_HEAD, blockdiag, hr == hc, levels)
    sls = [slice(LANES * pi, LANES * (pi + 1)) for pi in range(n_pairs)]
    args = [(r_ref[0, :, sl], k_ref[0, :, sl], v_ref[0, :, sl], lw_ref[0, :, sl], a_ref[0, :, sl],
             g_ref[0, :, sl], kk_ref[:, sl], ka_ref[:, sl], rk_ref[:, sl], lnw_ref[:, sl], lnb_ref[:, sl],
             h_scr[pi]) for pi, sl in enumerate(sls)]
    res = _rw_chunk(args, masks, t_valid)
    for pi, sl in enumerate(sls):
        o_ref[0, :, sl] = res[pi][0].astype(o_ref.dtype)
        h_scr[pi] = res[pi][1]

    @pl.when(ci == pl.num_programs(2) - 1)
    def _():
        hout_ref[0] = h_scr[...]


def _rw_scan(r, k, v, lw, a, g, kk, ka, rk, lnw, lnb, h0, t_valid, out_dtype):
    B, T, _ = r.shape
    C = min(T, 128)
    pb = N_RW_PAIRS
    w = LANES * pb
    tok = pl.BlockSpec((1, C, w), lambda b, p, c: (b, c, p))
    vec = pl.BlockSpec((1, w), lambda b, p, c: (0, p))
    st = pl.BlockSpec((1, pb, LANES, LANES), lambda b, p, c: (b, p, 0, 0))
    return pl.pallas_call(
        functools.partial(_rw_scan_kernel, t_valid=t_valid),
        grid=(B, N_RW_PAIRS // pb, T // C),
        in_specs=[tok] * 6 + [vec] * 5 + [st],
        out_specs=[tok, st],
        out_shape=[jax.ShapeDtypeStruct((B, T, RW_WIDTH), out_dtype),
                   jax.ShapeDtypeStruct((B, N_RW_PAIRS, LANES, LANES), F32)],
        scratch_shapes=[pltpu.VMEM((pb, LANES, LANES), F32)],
        compiler_params=_cp("parallel", "parallel", "arbitrary"),
        name="rwkv_scan",
    )(r, k, v, lw, a, g, kk, ka, rk, lnw, lnb, h0)


def _out_proj_kernel(a1_ref, a2_ref, w_ref, res_ref, o_ref):
    acc = _dot(a1_ref[...].astype(BF16), w_ref[:DA_WIDTH, :])
    acc = acc + _dot(a2_ref[...].astype(BF16), w_ref[DA_WIDTH:, :])
    o_ref[...] = res_ref[...] + acc


def _out_proj(a1, a2, w, res):
    M, D = res.shape
    tm = min(M, 1024)
    tn = 512
    return pl.pallas_call(
        _out_proj_kernel,
        grid=(M // tm, D // tn),
        in_specs=[pl.BlockSpec((tm, DA_WIDTH), lambda i, j: (i, 0)),
                  pl.BlockSpec((tm, RW_WIDTH), lambda i, j: (i, 0)),
                  pl.BlockSpec((DA_WIDTH + RW_WIDTH, tn), lambda i, j: (0, j)),
                  pl.BlockSpec((tm, tn), lambda i, j: (i, j))],
        out_specs=pl.BlockSpec((tm, tn), lambda i, j: (i, j)),
        out_shape=jax.ShapeDtypeStruct((M, D), F32),
        compiler_params=_cp("parallel", "parallel"),
        name="out_proj",
    )(a1, a2, w, res)


def _ffn_up_kernel(x_ref, g_ref, wg_ref, wu_ref, o_ref, xn_ref):
    @pl.when(pl.program_id(1) == 0)
    def _():
        x = x_ref[...]
        ms = jnp.mean(x * x, axis=-1, keepdims=True)
        xn_ref[...] = (x * lax.rsqrt(ms + NORM_EPS) * g_ref[...]).astype(BF16)

    xn = xn_ref[...]
    gate = _dot(xn, wg_ref[...])
    up = _dot(xn, wu_ref[...])
    o_ref[...] = (gate * jax.nn.sigmoid(gate) * up).astype(o_ref.dtype)


def _ffn_up(x, g, wg, wu):
    M, D = x.shape
    tm = min(M, 1024)
    tn = 512
    return pl.pallas_call(
        _ffn_up_kernel,
        grid=(M // tm, D_FF // tn),
        in_specs=[pl.BlockSpec((tm, D), lambda i, j: (i, 0)),
                  pl.BlockSpec((1, D), lambda i, j: (0, 0)),
                  pl.BlockSpec((D, tn), lambda i, j: (0, j)),
                  pl.BlockSpec((D, tn), lambda i, j: (0, j))],
        out_specs=pl.BlockSpec((tm, tn), lambda i, j: (i, j)),
        out_shape=jax.ShapeDtypeStruct((M, D_FF), BF16),
        scratch_shapes=[pltpu.VMEM((tm, D), BF16)],
        compiler_params=_cp("parallel", "arbitrary"),
        name="ffn_gate_up",
    )(x, g, wg, wu)


def _ffn_down_kernel(a_ref, w_ref, res_ref, o_ref):
    o_ref[...] = res_ref[...] + _dot(a_ref[...], w_ref[...])


def _ffn_down(a, w, res):
    M, D = res.shape
    tm = min(M, 1024)
    tn = 512
    return pl.pallas_call(
        _ffn_down_kernel,
        grid=(M // tm, D // tn),
        in_specs=[pl.BlockSpec((tm, D_FF), lambda i, j: (i, 0)),
                  pl.BlockSpec((D_FF, tn), lambda i, j: (0, j)),
                  pl.BlockSpec((tm, tn), lambda i, j: (i, j))],
        out_specs=pl.BlockSpec((tm, tn), lambda i, j: (i, j)),
        out_shape=jax.ShapeDtypeStruct((M, D), F32),
        compiler_params=_cp("parallel", "parallel"),
        name="ffn_down",
    )(a, w, res)


def _rope_tables(pos):
    half = DA_QK_DIM // 2
    inv_freq = jnp.power(ROPE_THETA, -jnp.arange(half, dtype=F32) / half)
    ang = pos.astype(F32)[:, None] * inv_freq[None, :]
    cos = jnp.cos(ang)
    sin = jnp.sin(ang)
    reps = LANES // DA_QK_DIM
    cos_t = jnp.tile(jnp.concatenate([cos, cos], axis=-1), (1, reps))
    sin_t = jnp.tile(jnp.concatenate([-sin, sin], axis=-1), (1, reps))
    return cos_t, sin_t


def _state_to_pairs(S):
    B = S.shape[0]
    St = jnp.swapaxes(S, -1, -2).reshape(B, N_RW_PAIRS, 2, RW_HEAD, RW_HEAD)
    eye2 = jnp.eye(2, dtype=S.dtype)
    Hp = St[:, :, :, :, None, :] * eye2[None, None, :, None, :, None]
    return Hp.reshape(B, N_RW_PAIRS, LANES, LANES)


def _pairs_to_state(Hp):
    B = Hp.shape[0]
    H5 = Hp.reshape(B, N_RW_PAIRS, 2, RW_HEAD, 2, RW_HEAD)
    diag = jnp.stack([H5[:, :, 0, :, 0, :], H5[:, :, 1, :, 1, :]], axis=2)
    return jnp.swapaxes(diag.reshape(B, N_RW_HEADS, RW_HEAD, RW_HEAD), -1, -2)


def _decode_query_rows(qb):
    Bd = qb.shape[0]
    sub = jnp.arange(DA_WIDTH) // DA_QK_DIM
    want = 2 * jnp.arange(N_DA_HEADS)[:, None] + jnp.arange(2)[None, :]
    mask = sub[None, None, :] == want[:, :, None]
    rows = jnp.where(mask[None, :, :, None, :], qb[:, None, None, :, :], jnp.zeros((), qb.dtype))
    return rows.reshape(Bd, -1, DA_WIDTH)


def _decode_head_rows(o, n_new):
    Bd = o.shape[0]
    o5 = o.reshape(Bd, N_DA_HEADS, 2, n_new, DA_V_DIM)[:, :, 0]
    return jnp.transpose(o5, (0, 2, 1, 3)).reshape(Bd, n_new, DA_WIDTH)


def _pad_rows(x, n):
    return jnp.pad(x, ((0, 0), (0, n - x.shape[1]), (0, 0)))


def _layer(x3, cos, sin, S0, shift0, lam, lam_init, p, decode=None):
    B, T, D = x3.shape
    M = B * T
    x = x3.reshape(M, D)
    hda = _norm_mm(x, p['norm_mix'], p['w_da'], 1024)
    hrw = _norm_mm(x, p['norm_mix'], p['w_rw'], 1152)
    qb, kf, kb, vb, vf = _qk_post(hda, p['q_gain'], p['k_gain'], cos, sin, T)
    v_new = vf.reshape(B, T, DA_WIDTH)
    if decode is None:
        o_da = _attn_prompt(lam, qb.reshape(B, T, DA_WIDTH), kb.reshape(B, T, DA_WIDTH),
                            vb.reshape(B, T, DA_WIDTH), p['subln_gain'], lam_init)
    else:
        page_table, cache_kt, cache_v4, layer = decode
        v_new_h = jnp.transpose(v_new.reshape(B, T, N_DA_HEADS, DA_V_DIM), (0, 2, 1, 3))
        v_new_h = jnp.pad(v_new_h, ((0, 0), (0, 0), (0, NEW_PAD - T), (0, 0)))
        o_da = _attn_sample(page_table, lam, _decode_query_rows(qb.reshape(B, T, DA_WIDTH)),
                            _pad_rows(kf.reshape(B, T, DA_WIDTH), NEW_PAD), v_new_h, p['subln_gain'],
                            cache_kt, cache_v4, layer, lam_init)
        o_da = _decode_head_rows(o_da, T)
    hrw3 = hrw.reshape(B, T, N_RW_PAD)
    shift_new = hrw3[:, -1, :N_RW_COLS]
    t_pad = max(T, 8)
    hrw3p = _pad_rows(hrw3, t_pad)
    sh0 = jnp.pad(shift0, ((0, 0), (0, N_RW_PAD - N_RW_COLS)))[:, None, :]
    r, k, v, lw, a, g = _rw_prep(hrw3p, sh0, p['mu_shift'], p['w0'], p['a0'], p['w_up'], p['a_up'],
                                 p['g_up'])
    o_rw, hp = _rw_scan(r, k, v, lw, a, g, p['k_k'], p['k_a'], p['r_k'], p['lnx_w'], p['lnx_b'],
                        _state_to_pairs(S0), T, BF16 if decode is None else F32)
    o_rw = o_rw[:, :T]
    S_new = _pairs_to_state(hp)
    x1 = _out_proj(o_da.reshape(M, DA_WIDTH), o_rw.reshape(M, RW_WIDTH), p['w_out'], x)
    ff = _ffn_up(x1, p['norm_ffn'], p['w_gate'], p['w_up_ffn'])
    x2 = _ffn_down(ff, p['w_down'], x1)
    return x2.reshape(B, T, D), kf, v_new, S_new, shift_new


def kernel(x_prompt, x_sample, cache_k, cache_v, state_wkv, state_shift, page_table, norm_mix, w_in, q_gain, k_gain, lambda_q1, lambda_k1, lambda_q2, lambda_k2, subln_gain, mu_shift, w0, w_up, a0, a_up, g_up, k_k, k_a, r_k, lnx_w, lnx_b, w_out, norm_ffn, w_gate, w_up_ffn, w_down):
    B, S, _ = x_prompt.shape
    Bd, T, _ = x_sample.shape
    depth = w_in.shape[0]
    n_pages = page_table.shape[1]
    past_len = n_pages * PAGE_SIZE
    cos_p, sin_p = _rope_tables(jnp.arange(S, dtype=jnp.int32))
    cos_s, sin_s = _rope_tables(past_len + jnp.arange(T, dtype=jnp.int32))
    cos_s = jnp.tile(cos_s, (Bd, 1))
    sin_s = jnp.tile(sin_s, (Bd, 1))
    n_pool = cache_k.shape[1]
    cache_kt = jnp.transpose(cache_k, (0, 1, 3, 4, 2)).reshape(depth, n_pool, DA_WIDTH, PAGE_SIZE)
    cache_v4 = cache_v.reshape(depth, n_pool, PAGE_SIZE * N_DA_HEADS, DA_V_DIM)
    row = lambda t: t.reshape(1, -1)
    tile_gain = lambda t: jnp.tile(t, LANES // DA_QK_DIM).reshape(1, LANES)

    xp, xs = x_prompt, x_sample
    outs = [[] for _ in range(8)]
    for l in range(depth):
        lam_init = 0.8 - 0.6 * math.exp(-0.3 * l)
        lam = (jnp.exp(jnp.sum(lambda_q1[l] * lambda_k1[l])) - jnp.exp(jnp.sum(lambda_q2[l] * lambda_k2[l]))
               + lam_init).reshape(1).astype(F32)
        w_in_b = w_in[l].astype(BF16)
        p = {
            'norm_mix': row(norm_mix[l]),
            'w_da': w_in_b[:, :3 * DA_WIDTH],
            'w_rw': jnp.pad(w_in_b[:, 3 * DA_WIDTH:], ((0, 0), (0, N_RW_PAD - N_RW_COLS))),
            'q_gain': tile_gain(q_gain[l]), 'k_gain': tile_gain(k_gain[l]),
            'subln_gain': row(subln_gain[l]),
            'mu_shift': jnp.pad(row(mu_shift[l]), ((0, 0), (0, N_RW_PAD - N_RW_COLS))),
            'w0': row(w0[l]), 'a0': row(a0[l]),
            'w_up': jnp.pad(w_up[l], ((0, LANES - W_LORA), (0, 0))),
            'a_up': jnp.pad(a_up[l], ((W_LORA, LANES - W_LORA - A_LORA), (0, 0))),
            'g_up': jnp.pad(g_up[l], ((0, G_PAD - G_LORA), (0, 0))),
            'k_k': row(k_k[l]), 'k_a': row(k_a[l]), 'r_k': row(r_k[l]),
            'lnx_w': row(lnx_w[l]), 'lnx_b': row(lnx_b[l]),
            'w_out': w_out[l].astype(BF16), 'norm_ffn': row(norm_ffn[l]),
            'w_gate': w_gate[l].astype(BF16), 'w_up_ffn': w_up_ffn[l].astype(BF16),
            'w_down': w_down[l].astype(BF16),
        }
        S0_p = jnp.zeros((B, N_RW_HEADS, RW_HEAD, RW_HEAD), F32)
        sh0_p = jnp.zeros((B, N_RW_COLS), F32)
        xp, k_new, v_new, S_new, sh_new = _layer(xp, cos_p, sin_p, S0_p, sh0_p, lam, lam_init, p)
        outs[0].append(k_new.reshape(B, S // PAGE_SIZE, PAGE_SIZE, N_DA_SUB, DA_QK_DIM))
        outs[1].append(v_new.reshape(B, S // PAGE_SIZE, PAGE_SIZE, N_DA_HEADS, DA_V_DIM))
        outs[2].append(S_new)
        outs[3].append(sh_new)
        xs, k_new, v_new, S_new, sh_new = _layer(xs, cos_s, sin_s, state_wkv[l], state_shift[l], lam,
                                                 lam_init, p, decode=(page_table, cache_kt, cache_v4, l))
        outs[4].append(k_new.reshape(Bd, T, N_DA_SUB, DA_QK_DIM))
        outs[5].append(v_new.reshape(Bd, T, N_DA_HEADS, DA_V_DIM))
        outs[6].append(S_new)
        outs[7].append(sh_new)
    return (xp, xs) + tuple(jnp.stack(o) for o in outs)
```
